```python
import jax, jax.numpy as jnp
from jax import lax
import numpy as np

D_MODEL = 1024
BATCH = 8
SEQ = 2048
DEPTH = 1
DEC_BATCH = 16
DEC_SEQ = 64
PAST_LEN = 2048

CHUNK = 64
HEAD_DIM = 64
N_HEADS = 8
N_KV_HEADS = 2
GROUP = N_HEADS // N_KV_HEADS
ATTN_WIDTH = N_HEADS * HEAD_DIM
KV_WIDTH = N_KV_HEADS * HEAD_DIM
WINDOW = 128
WIN_CACHE = min(WINDOW, PAST_LEN)
ROPE_THETA = 10000.0
RET_HEADS = 4
RET_HEAD_DIM = 128
RET_WIDTH = RET_HEADS * RET_HEAD_DIM
RET_THETA = 10000.0
MIX_WIDTH = ATTN_WIDTH + RET_WIDTH
D_FF = 4 * D_MODEL
EPS = 1e-6
SPLITS = [ATTN_WIDTH,
          ATTN_WIDTH + KV_WIDTH,
          ATTN_WIDTH + 2 * KV_WIDTH,
          ATTN_WIDTH + 2 * KV_WIDTH + RET_WIDTH,
          ATTN_WIDTH + 2 * KV_WIDTH + 2 * RET_WIDTH,
          ATTN_WIDTH + 2 * KV_WIDTH + 3 * RET_WIDTH]
IN_WIDTH = ATTN_WIDTH + 2 * KV_WIDTH + 4 * RET_WIDTH

kernel_name = 'hymba_swa_sink_retention_stream_step'


def _rmsnorm(x, gain):
    xf = x.astype(jnp.float32)
    y = xf * lax.rsqrt(jnp.mean(xf * xf, axis=-1, keepdims=True) + EPS)
    return (y * gain.astype(jnp.float32)).astype(x.dtype)


def _attn_inv_freq():
    return 1.0 / (ROPE_THETA ** (jnp.arange(0, HEAD_DIM, 2, dtype=jnp.float32) / HEAD_DIM))


def _ret_inv_freq():
    return 1.0 / (RET_THETA ** jnp.linspace(0.0, 1.0, RET_HEAD_DIM // 2, dtype=jnp.float32))


def _ret_log_decay():
    return jnp.log1p(-jnp.exp2(-5.0 - jnp.arange(RET_HEADS, dtype=jnp.float32)))


def _rotate(x, pos, inv_freq):
    ang = pos.astype(jnp.float32)[:, None] * inv_freq[None, :]
    cos = jnp.cos(ang)[None, :, None, :]
    sin = jnp.sin(ang)[None, :, None, :]
    xf = x.astype(jnp.float32)
    x1, x2 = jnp.split(xf, 2, axis=-1)
    return jnp.concatenate([x1 * cos - x2 * sin, x2 * cos + x1 * sin], axis=-1).astype(x.dtype)


def _project(a, w_in, pos):
    B, L, _ = a.shape
    z = a @ w_in
    qa, ka, va, qr, kr, vr, gr = jnp.split(z, SPLITS, axis=-1)
    qa = _rotate(qa.reshape(B, L, N_HEADS, HEAD_DIM), pos, _attn_inv_freq())
    ka = _rotate(ka.reshape(B, L, N_KV_HEADS, HEAD_DIM), pos, _attn_inv_freq())
    va = va.reshape(B, L, N_KV_HEADS, HEAD_DIM)
    qr = _rotate(qr.reshape(B, L, RET_HEADS, RET_HEAD_DIM), pos, _ret_inv_freq())
    kr = _rotate(kr.reshape(B, L, RET_HEADS, RET_HEAD_DIM), pos, _ret_inv_freq()) * (RET_HEAD_DIM ** -0.5)
    vr = vr.reshape(B, L, RET_HEADS, RET_HEAD_DIM)
    return qa, ka, va, qr, kr, vr, gr


def _attend(q, k, v, sinks, mask):
    s = jnp.einsum('...qkgd,...skd->...kgqs', q, k).astype(jnp.float32) * (HEAD_DIM ** -0.5)
    if mask is not None:
        s = jnp.where(mask, s, -jnp.inf)
    sink = sinks.astype(jnp.float32).reshape(N_KV_HEADS, GROUP, 1)
    m = jnp.maximum(jnp.max(s, axis=-1), sink)
    p = jnp.exp(s - m[..., None])
    denom = jnp.sum(p, axis=-1) + jnp.exp(sink - m)
    p = (p / denom[..., None]).astype(v.dtype)
    o = jnp.einsum('...kgqs,...skd->...qkgd', p, v)
    return o.reshape(o.shape[:-3] + (ATTN_WIDTH,))


def _swa_prompt(q, k, v, sinks):
    B, S = q.shape[:2]
    n_c = S // CHUNK
    P = WINDOW // CHUNK
    qb = q.reshape(B, n_c, CHUNK, N_KV_HEADS, GROUP, HEAD_DIM)

    def band(t):
        tp = jnp.pad(t, ((0, 0), (WINDOW, 0), (0, 0), (0, 0)))
        tp = tp.reshape(B, n_c + P, CHUNK, N_KV_HEADS, HEAD_DIM)
        return jnp.concatenate([tp[:, j:j + n_c] for j in range(P + 1)], axis=2)

    kb, vb = band(k), band(v)
    key_pos = (jnp.arange(n_c)[:, None] - P) * CHUNK + jnp.arange((P + 1) * CHUNK)[None, :]
    mask = (key_pos >= 0)[:, None, None, None, :]
    o = _attend(qb, kb, vb, sinks, mask)
    return o.reshape(B, S, ATTN_WIDTH)


def _retention_chunk(state, q, k, v, log_g):
    L = q.shape[1]
    idx = jnp.arange(L, dtype=jnp.float32)
    diff = idx[:, None] - idx[None, :]
    decay = jnp.where(diff >= 0, jnp.exp(log_g[:, None, None] * jnp.maximum(diff, 0.0)), 0.0)
    sc = jnp.einsum('bihd,bjhd->bhij', q, k) * decay[None]
    intra = jnp.einsum('bhij,bjhe->bihe', sc, v)
    q_dec = jnp.exp(log_g[None, :] * (idx[:, None] + 1.0))
    cross = jnp.einsum('bihd,bhde->bihe', q, state) * q_dec[None, :, :, None]
    k_dec = jnp.exp(log_g[None, :] * (L - 1.0 - idx[:, None]))
    new_state = (jnp.exp(log_g * L)[None, :, None, None] * state
                 + jnp.einsum('bjhd,bjhe->bhde', k * k_dec[None, :, :, None], v))
    return intra + cross, new_state


def _retention_prompt(q, k, v):
    B, S, H, D = q.shape
    n_c = S // CHUNK
    log_g = _ret_log_decay()

    def to_chunks(t):
        return jnp.moveaxis(t.astype(jnp.float32).reshape(B, n_c, CHUNK, H, t.shape[-1]), 1, 0)

    def step(state, qkv):
        qc, kc, vc = qkv
        o, state = _retention_chunk(state, qc, kc, vc, log_g)
        return state, o

    s0 = jnp.zeros((B, H, D, RET_HEAD_DIM), jnp.float32)
    s_fin, o = lax.scan(step, s0, (to_chunks(q), to_chunks(k), to_chunks(v)))
    return jnp.moveaxis(o, 0, 1).reshape(B, S, H, RET_HEAD_DIM), s_fin


def _mix_out(attn_o, ret_o, gr, w_out):
    B, L = attn_o.shape[:2]
    r = ret_o * lax.rsqrt(jnp.mean(ret_o * ret_o, axis=-1, keepdims=True) + EPS)
    r = r.reshape(B, L, RET_WIDTH) * jax.nn.silu(gr.astype(jnp.float32))
    mixed = jnp.concatenate([attn_o, r.astype(attn_o.dtype)], axis=-1)
    return mixed @ w_out


def _sqrelu_mlp(a, w_up, w_down):
    u = jax.nn.relu(a @ w_up)
    return (u * u) @ w_down


def setup_inputs(seed: int = 0) -> dict:
    key = jax.random.key(seed)
    ks = jax.random.split(key, 16)
    f32 = jnp.float32
    x_prompt = jax.random.normal(ks[0], (BATCH, SEQ, D_MODEL), f32)
    x_sample = jax.random.normal(ks[1], (DEC_BATCH, DEC_SEQ, D_MODEL), f32)
    cache_k = jax.random.normal(ks[2], (DEPTH, DEC_BATCH, WIN_CACHE, N_KV_HEADS, HEAD_DIM), f32)
    cache_v = jax.random.normal(ks[3], (DEPTH, DEC_BATCH, WIN_CACHE, N_KV_HEADS, HEAD_DIM), f32)
    state_ret = jax.random.normal(ks[4], (DEPTH, DEC_BATCH, RET_HEADS, RET_HEAD_DIM, RET_HEAD_DIM), f32)
    norm1 = 1.0 + 0.05 * jax.random.normal(ks[5], (DEPTH, D_MODEL), f32)
    w_in = jax.random.normal(ks[6], (DEPTH, D_MODEL, IN_WIDTH), f32) * D_MODEL ** -0.5
    sinks = 0.5 * jax.random.normal(ks[7], (DEPTH, N_HEADS), f32)
    w_out = jax.random.normal(ks[8], (DEPTH, MIX_WIDTH, D_MODEL), f32) * MIX_WIDTH ** -0.5
    norm2 = 1.0 + 0.05 * jax.random.normal(ks[9], (DEPTH, D_MODEL), f32)
    w_up = jax.random.normal(ks[10], (DEPTH, D_MODEL, D_FF), f32) * D_MODEL ** -0.5
    w_down = jax.random.normal(ks[11], (DEPTH, D_FF, D_MODEL), f32) * D_FF ** -0.5
    norm_f = 1.0 + 0.05 * jax.random.normal(ks[12], (D_MODEL,), f32)
    return {'x_prompt': x_prompt, 'x_sample': x_sample, 'cache_k': cache_k, 'cache_v': cache_v,
            'state_ret': state_ret, 'norm1': norm1, 'w_in': w_in, 'sinks': sinks, 'w_out': w_out,
            'norm2': norm2, 'w_up': w_up, 'w_down': w_down, 'norm_f': norm_f}


def reference(x_prompt, x_sample, cache_k, cache_v, state_ret, norm1, w_in, sinks, w_out,
              norm2, w_up, w_down, norm_f):
    hp, hs = x_prompt, x_sample
    pos_p = jnp.arange(x_prompt.shape[1])
    pos_s = PAST_LEN + jnp.arange(x_sample.shape[1])
    nk_p, nv_p, ns_p, nk_s, nv_s, ns_s = [], [], [], [], [], []
    for l in range(DEPTH):
        a = _rmsnorm(hp, norm1[l])
        qa, ka, va, qr, kr, vr, gr = _project(a, w_in[l], pos_p)
        attn_o = _swa_prompt(qa, ka, va, sinks[l])
        ret_o, s_p = _retention_prompt(qr, kr, vr)
        hp = hp + _mix_out(attn_o, ret_o, gr, w_out[l])
        hp = hp + _sqrelu_mlp(_rmsnorm(hp, norm2[l]), w_up[l], w_down[l])
        nk_p.append(ka[:, -WIN_CACHE:])
        nv_p.append(va[:, -WIN_CACHE:])
        ns_p.append(s_p)
        a = _rmsnorm(hs, norm1[l])
        qa, ka, va, qr, kr, vr, gr = _project(a, w_in[l], pos_s)
        B, L = hs.shape[:2]
        k_all = jnp.concatenate([cache_k[l].astype(ka.dtype), ka], axis=1)
        v_all = jnp.concatenate([cache_v[l].astype(va.dtype), va], axis=1)
        attn_o = _attend(qa.reshape(B, L, N_KV_HEADS, GROUP, HEAD_DIM), k_all, v_all, sinks[l], None)
        ret_o, s_s = _retention_chunk(state_ret[l].astype(jnp.float32), qr.astype(jnp.float32),
                                      kr.astype(jnp.float32), vr.astype(jnp.float32), _ret_log_decay())
        hs = hs + _mix_out(attn_o, ret_o, gr, w_out[l])
        hs = hs + _sqrelu_mlp(_rmsnorm(hs, norm2[l]), w_up[l], w_down[l])
        nk_s.append(k_all[:, -WIN_CACHE:])
        nv_s.append(v_all[:, -WIN_CACHE:])
        ns_s.append(s_s)
    y_prompt = _rmsnorm(hp, norm_f)
    y_sample = _rmsnorm(hs, norm_f)
    return (y_prompt, y_sample,
            jnp.stack(nk_p, 0), jnp.stack(nv_p, 0), jnp.stack(ns_p, 0),
            jnp.stack(nk_s, 0), jnp.stack(nv_s, 0), jnp.stack(ns_s, 0))
```

```python
import functools

import jax
import jax.numpy as jnp
from jax import lax
from jax.experimental import pallas as pl
from jax.experimental.pallas import tpu as pltpu

CHUNK = 64
HEAD_DIM = 64
N_HEADS = 8
N_KV_HEADS = 2
GROUP = N_HEADS // N_KV_HEADS
ATTN_WIDTH = N_HEADS * HEAD_DIM
KV_WIDTH = N_KV_HEADS * HEAD_DIM
WINDOW = 128
PAST_LEN = 2048
ROPE_THETA = 10000.0
RET_HEADS = 4
RET_HEAD_DIM = 128
RET_WIDTH = RET_HEADS * RET_HEAD_DIM
RET_THETA = 10000.0
EPS = 1e-6

LANES = 128
VMEM_LIMIT_BYTES = 56 * 1024 * 1024

SEC_A = ATTN_WIDTH + KV_WIDTH
SEC_R = 2 * RET_WIDTH
SEC_P = KV_WIDTH + 2 * RET_WIDTH
HEAD_PERM = tuple(kv * GROUP + g for g in range(GROUP) for kv in range(N_KV_HEADS))

BF16 = jnp.bfloat16
F32 = jnp.float32


def _rms_scale(x, gain):
    return x * lax.rsqrt(jnp.mean(x * x, axis=-1, keepdims=True) + EPS) * gain


def _in_proj_kernel(x_ref, gain_ref, w_ref, cosa_ref, sina_ref, cosr_ref, sinr_ref,
                    qa_ref, k32_ref, v32_ref, k0_ref, k1_ref, v0_ref, v1_ref,
                    qr_ref, kr_ref, vr_ref, gr_ref):
    tm = x_ref.shape[0]
    a = _rms_scale(x_ref[...], gain_ref[...]).astype(BF16)
    lane = lax.broadcasted_iota(jnp.int32, (tm, LANES), 1)
    kv0_lanes = lane < HEAD_DIM

    za = jnp.dot(a, w_ref[:, 0:SEC_A], preferred_element_type=F32)
    first_half = (lane % HEAD_DIM) < (HEAD_DIM // 2)
    cosa, sina = cosa_ref[...], sina_ref[...]
    for j in range(SEC_A // LANES):
        xb = za[:, j * LANES:(j + 1) * LANES]
        partner = jnp.where(first_half, pltpu.roll(xb, LANES - HEAD_DIM // 2, 1),
                            pltpu.roll(xb, HEAD_DIM // 2, 1))
        rot = xb * cosa + partner * sina
        if j < ATTN_WIDTH // LANES:
            qa_ref[:, j * LANES:(j + 1) * LANES] = (rot * (HEAD_DIM ** -0.5)).astype(BF16)
        else:
            k32_ref[...] = rot
            k0_ref[...] = jnp.where(kv0_lanes, rot, 0.0).astype(BF16)
            k1_ref[...] = jnp.where(kv0_lanes, 0.0, rot).astype(BF16)

    zr = jnp.dot(a, w_ref[:, SEC_A:SEC_A + SEC_R], preferred_element_type=F32)
    cosr, sinr = cosr_ref[...], sinr_ref[...]
    for j in range(SEC_R // LANES):
        xb = zr[:, j * LANES:(j + 1) * LANES]
        rot = xb * cosr + pltpu.roll(xb, RET_HEAD_DIM // 2, 1) * sinr
        if j < RET_HEADS:
            qr_ref[:, j * LANES:(j + 1) * LANES] = rot.astype(BF16)
        else:
            jj = j - RET_HEADS
            kr_ref[:, jj * LANES:(jj + 1) * LANES] = (rot * (RET_HEAD_DIM ** -0.5)).astype(BF16)

    zp = jnp.dot(a, w_ref[:, SEC_A + SEC_R:], preferred_element_type=F32)
    v = zp[:, 0:KV_WIDTH]
    v32_ref[...] = v
    v0_ref[...] = jnp.where(kv0_lanes, v, 0.0).astype(BF16)
    v1_ref[...] = jnp.where(kv0_lanes, 0.0, v).astype(BF16)
    vr_ref[...] = zp[:, KV_WIDTH:KV_WIDTH + RET_WIDTH].astype(BF16)
    gr_ref[...] = zp[:, KV_WIDTH + RET_WIDTH:].astype(BF16)


def _in_proj(x2d, gain, w_cat, tables, tm, table_blocks):
    n, d = x2d.shape
    row = lambda i: (i, 0)
    const = lambda i: (0, 0)
    tab = lambda i: (i % table_blocks, 0)
    wide = lambda dt: jax.ShapeDtypeStruct((n, ATTN_WIDTH), dt)
    narrow = lambda dt: jax.ShapeDtypeStruct((n, KV_WIDTH), dt)
    out_shape = [wide(BF16), narrow(F32), narrow(F32), narrow(BF16), narrow(BF16), narrow(BF16),
                 narrow(BF16), wide(BF16), wide(BF16), wide(BF16), wide(BF16)]
    out_specs = [pl.BlockSpec((tm, s.shape[1]), row) for s in out_shape]
    in_specs = [pl.BlockSpec((tm, d), row), pl.BlockSpec((1, d), const),
                pl.BlockSpec(w_cat.shape, const)] + [pl.BlockSpec((tm, LANES), tab)] * 4
    return pl.pallas_call(
        _in_proj_kernel, grid=(n // tm,), in_specs=in_specs, out_specs=out_specs, out_shape=out_shape,
        compiler_params=pltpu.CompilerParams(dimension_semantics=("parallel",),
                                             vmem_limit_bytes=VMEM_LIMIT_BYTES),
        name="in_proj")(x2d, gain, w_cat, *tables)


def _attend_chunk(q, k0, k1, v0, v1, sink_ref, valid):
    q4 = jnp.concatenate([q[:, g * LANES:(g + 1) * LANES] for g in range(GROUP)], axis=0)
    out = None
    for kv, (k, v) in enumerate(((k0, v0), (k1, v1))):
        s = lax.dot_general(q4, k, (((1,), (1,)), ((), ())), preferred_element_type=F32)
        if valid is not None:
            s = jnp.where(valid, s, -jnp.inf)
        sink = sink_ref[kv]
        m = jnp.maximum(jnp.max(s, axis=-1, keepdims=True), sink)
        p = jnp.exp(s - m)
        denom = jnp.sum(p, axis=-1, keepdims=True) + jnp.exp(sink - m)
        o = jnp.dot(p.astype(BF16), v, preferred_element_type=F32) * (1.0 / denom)
        out = o if out is None else out + o
    return out


def _store_attn(mixed_ref, row0, o):
    for g in range(GROUP):
        mixed_ref[row0:row0 + CHUNK, g * LANES:(g + 1) * LANES] = (
            o[g * CHUNK:(g + 1) * CHUNK, :].astype(mixed_ref.dtype))


def _retention_block(qr_ref, kr_ref, vr_ref, gr_ref, dmask_ref, qdec_ref, kdec_ref, gl_ref,
                     get_state, set_state, mixed_ref):
    for h in range(RET_HEADS):
        cols = slice(h * RET_HEAD_DIM, (h + 1) * RET_HEAD_DIM)
        qh, kh, vh = qr_ref[:, cols], kr_ref[:, cols], vr_ref[:, cols]
        sc = lax.dot_general(qh, kh, (((1,), (1,)), ((), ())), preferred_element_type=F32)
        sc = sc * dmask_ref[h]
        intra = jnp.dot(sc.astype(BF16), vh, preferred_element_type=F32)
        state = get_state(h)
        cross = jnp.dot(qh, state.astype(BF16), preferred_element_type=F32) * qdec_ref[:, cols]
        o = intra + cross
        kd = (kh.astype(F32) * kdec_ref[:, cols]).astype(BF16)
        kv = lax.dot_general(kd, vh, (((0,), (0,)), ((), ())), preferred_element_type=F32)
        set_state(h, gl_ref[h] * state + kv)
        r = o * lax.rsqrt(jnp.mean(o * o, axis=-1, keepdims=True) + EPS)
        g = gr_ref[:, cols].astype(F32)
        gate = g / (1.0 + jnp.exp(-g))
        mixed_ref[:, ATTN_WIDTH + h * RET_HEAD_DIM:ATTN_WIDTH + (h + 1) * RET_HEAD_DIM] = (
            (r * gate).astype(mixed_ref.dtype))


def _mixer_prompt_kernel(qa_ref, k0_ref, k1_ref, v0_ref, v1_ref, sink_ref,
                         qr_ref, kr_ref, vr_ref, gr_ref, dmask_ref, qdec_ref, kdec_ref, gl_ref,
                         mixed_ref, state_out_ref, state_scr):
    j = pl.program_id(1)
    tq = qa_ref.shape[0]
    chunks = tq // CHUNK
    span = WINDOW + CHUNK
    past = WINDOW // CHUNK

    @pl.when(j == 0)
    def _():
        state_scr[...] = jnp.zeros_like(state_scr)

    key_idx = lax.broadcasted_iota(jnp.int32, (GROUP * CHUNK, span), 1)
    for c in range(chunks):
        cg = j * chunks + c
        start = pl.multiple_of(jnp.maximum(cg - past, 0) * CHUNK, CHUNK)
        n_valid = jnp.minimum(cg + 1, past + 1) * CHUNK
        win = pl.ds(start, span)
        o = _attend_chunk(qa_ref[c * CHUNK:(c + 1) * CHUNK, :], k0_ref[win, :], k1_ref[win, :],
                          v0_ref[win, :], v1_ref[win, :], sink_ref, key_idx < n_valid)
        _store_attn(mixed_ref, c * CHUNK, o)

    def set_state(h, val):
        state_scr[h] = val

    _retention_block(qr_ref, kr_ref, vr_ref, gr_ref, dmask_ref, qdec_ref, kdec_ref, gl_ref,
                     lambda h: state_scr[h], set_state, mixed_ref)

    @pl.when(j == pl.num_programs(1) - 1)
    def _():
        state_out_ref[0] = state_scr[...]


def _mixer_sample_kernel(qa_ref, knew_ref, vnew_ref, kcache_ref, vcache_ref, state_in_ref, sink_ref,
                         qr_ref, kr_ref, vr_ref, gr_ref, dmask_ref, qdec_ref, kdec_ref, gl_ref,
                         mixed_ref, state_out_ref):
    k = jnp.concatenate([kcache_ref[0], knew_ref[...]], axis=0)
    v = jnp.concatenate([vcache_ref[0], vnew_ref[...]], axis=0)
    kv0_lanes = lax.broadcasted_iota(jnp.int32, k.shape, 1) < HEAD_DIM
    k0 = jnp.where(kv0_lanes, k, 0.0).astype(BF16)
    k1 = jnp.where(kv0_lanes, 0.0, k).astype(BF16)
    v0 = jnp.where(kv0_lanes, v, 0.0).astype(BF16)
    v1 = jnp.where(kv0_lanes, 0.0, v).astype(BF16)
    o = _attend_chunk(qa_ref[...], k0, k1, v0, v1, sink_ref, None)
    _store_attn(mixed_ref, 0, o)

    def set_state(h, val):
        state_out_ref[0, h] = val

    _retention_block(qr_ref, kr_ref, vr_ref, gr_ref, dmask_ref, qdec_ref, kdec_ref, gl_ref,
                     lambda h: state_in_ref[0, h], set_state, mixed_ref)


def _retention_tables(length):
    log_g = jnp.log1p(-jnp.exp2(-5.0 - jnp.arange(RET_HEADS, dtype=F32)))
    idx = jnp.arange(length, dtype=F32)
    diff = idx[:, None] - idx[None, :]
    dmask = jnp.where(diff >= 0, jnp.exp(log_g[:, None, None] * jnp.maximum(diff, 0.0)), 0.0)
    qdec = jnp.exp(log_g[None, :] * (idx[:, None] + 1.0))
    kdec = jnp.exp(log_g[None, :] * (length - 1.0 - idx[:, None]))
    gl = jnp.exp(log_g * length)
    return (dmask, jnp.repeat(qdec, RET_HEAD_DIM, axis=1), jnp.repeat(kdec, RET_HEAD_DIM, axis=1),
            jnp.broadcast_to(gl[:, None, None], (RET_HEADS, 1, RET_HEAD_DIM)))


def _const_spec(arr):
    nd = arr.ndim
    return pl.BlockSpec(arr.shape, lambda *_: (0,) * nd)


def _mixer_prompt(streams, sink_rows, batch, seq, tq):
    qa, _, _, k0, k1, v0, v1, qr, kr, vr, gr = streams
    nt = seq // tq
    tables = _retention_tables(tq)
    tile = lambda b, j: (b * nt + j, 0)
    per_batch = lambda b, j: (b, 0)
    wide = pl.BlockSpec((tq, ATTN_WIDTH), tile)
    kvspec = pl.BlockSpec((seq, KV_WIDTH), per_batch)
    in_specs = ([wide] + [kvspec] * 4 + [_const_spec(sink_rows)] + [wide] * 4
                + [_const_spec(t) for t in tables])
    mixed, state = pl.pallas_call(
        _mixer_prompt_kernel, grid=(batch, nt), in_specs=in_specs,
        out_specs=[pl.BlockSpec((tq, ATTN_WIDTH + RET_WIDTH), tile),
                   pl.BlockSpec((1, RET_HEADS, RET_HEAD_DIM, RET_HEAD_DIM), lambda b, j: (b, 0, 0, 0))],
        out_shape=[jax.ShapeDtypeStruct((batch * seq, ATTN_WIDTH + RET_WIDTH), BF16),
                   jax.ShapeDtypeStruct((batch, RET_HEADS, RET_HEAD_DIM, RET_HEAD_DIM), F32)],
        scratch_shapes=[pltpu.VMEM((RET_HEADS, RET_HEAD_DIM, RET_HEAD_DIM), F32)],
        compiler_params=pltpu.CompilerParams(dimension_semantics=("parallel", "arbitrary"),
                                             vmem_limit_bytes=VMEM_LIMIT_BYTES),
        name="mixer_prompt")(qa, k0, k1, v0, v1, sink_rows, qr, kr, vr, gr, *tables)
    return mixed, state


def _mixer_sample(streams, sink_rows, cache_k, cache_v, state_in, batch, length):
    qa, k32, v32, _, _, _, _, qr, kr, vr, gr = streams
    tables = _retention_tables(length)
    tile = lambda b: (b, 0)
    wide = pl.BlockSpec((length, ATTN_WIDTH), tile)
    new_kv = pl.BlockSpec((length, KV_WIDTH), tile)
    cache = pl.BlockSpec((1,) + cache_k.shape[1:], lambda b: (b, 0, 0))
    state_spec = pl.BlockSpec((1, RET_HEADS, RET_HEAD_DIM, RET_HEAD_DIM), lambda b: (b, 0, 0, 0))
    in_specs = ([wide, new_kv, new_kv, cache, cache, state_spec, _const_spec(sink_rows)] + [wide] * 4
                + [_const_spec(t) for t in tables])
    mixed, state = pl.pallas_call(
        _mixer_sample_kernel, grid=(batch,), in_specs=in_specs,
        out_specs=[pl.BlockSpec((length, ATTN_WIDTH + RET_WIDTH), tile), state_spec],
        out_shape=[jax.ShapeDtypeStruct((batch * length, ATTN_WIDTH + RET_WIDTH), BF16),
                   jax.ShapeDtypeStruct(state_in.shape, F32)],
        compiler_params=pltpu.CompilerParams(dimension_semantics=("parallel",),
                                             vmem_limit_bytes=VMEM_LIMIT_BYTES),
        name="mixer_sample")(qa, k32, v32, cache_k, cache_v, state_in, sink_rows, qr, kr, vr, gr, *tables)
    return mixed, state


def _out_mlp_kernel(x_ref, mixed_ref, wout_ref, g2_ref, wup_ref, wdown_ref, gf_ref, y_ref, *,
                    ff_block, final_norm):
    h = x_ref[...] + jnp.dot(mixed_ref[...], wout_ref[...], preferred_element_type=F32)
    a = _rms_scale(h, g2_ref[...]).astype(BF16)
    acc = h
    for c in range(wup_ref.shape[1] // ff_block):
        cols = slice(c * ff_block, (c + 1) * ff_block)
        u = jnp.maximum(jnp.dot(a, wup_ref[:, cols], preferred_element_type=F32), 0.0)
        acc = acc + jnp.dot((u * u).astype(BF16), wdown_ref[cols, :], preferred_element_type=F32)
    y_ref[...] = _rms_scale(acc, gf_ref[...]) if final_norm else acc


def _out_mlp(x2d, mixed, w_out, g2, w_up, w_down, gf, tm, ff_block, final_norm):
    n, d = x2d.shape
    row = lambda i: (i, 0)
    const = lambda i: (0, 0)
    resident = lambda arr: pl.BlockSpec(arr.shape, const, pipeline_mode=pl.Buffered(1))
    in_specs = [pl.BlockSpec((tm, d), row), pl.BlockSpec((tm, mixed.shape[1]), row), resident(w_out),
                pl.BlockSpec((1, d), const), resident(w_up), resident(w_down), pl.BlockSpec((1, d), const)]
    return pl.pallas_call(
        functools.partial(_out_mlp_kernel, ff_block=ff_block, final_norm=final_norm),
        grid=(n // tm,), in_specs=in_specs, out_specs=pl.BlockSpec((tm, d), row),
        out_shape=jax.ShapeDtypeStruct((n, d), F32),
        compiler_params=pltpu.CompilerParams(dimension_semantics=("parallel",),
                                             vmem_limit_bytes=VMEM_LIMIT_BYTES),
        name="out_mlp")(x2d, mixed, w_out, g2, w_up, w_down, gf)


def _rotary_tables(pos, head_dim, theta_exponents):
    inv_freq = 1.0 / theta_exponents
    ang = pos.astype(F32)[:, None] * inv_freq[None, :]
    cos, sin = jnp.cos(ang), jnp.sin(ang)
    reps = LANES // head_dim
    return (jnp.tile(jnp.concatenate([cos, cos], axis=-1), (1, reps)),
            jnp.tile(jnp.concatenate([-sin, sin], axis=-1), (1, reps)))


def _all_rotary_tables(pos):
    attn = _rotary_tables(pos, HEAD_DIM, ROPE_THETA ** (jnp.arange(0, HEAD_DIM, 2, dtype=F32) / HEAD_DIM))
    ret = _rotary_tables(pos, RET_HEAD_DIM, RET_THETA ** jnp.linspace(0.0, 1.0, RET_HEAD_DIM // 2, dtype=F32))
    return attn + ret


def _permute_heads(w, axis):
    shape = w.shape
    split = shape[:axis] + (N_HEADS, HEAD_DIM) + shape[axis + 1:]
    return jnp.take(w.reshape(split), jnp.array(HEAD_PERM), axis=axis).reshape(shape)


def _split_in_proj(w_in):
    bounds = [0, ATTN_WIDTH, ATTN_WIDTH + KV_WIDTH, ATTN_WIDTH + 2 * KV_WIDTH]
    bounds += [bounds[-1] + RET_WIDTH * i for i in range(1, 5)]
    return [w_in[:, lo:hi] for lo, hi in zip(bounds[:-1], bounds[1:])]


def kernel(x_prompt, x_sample, cache_k, cache_v, state_ret, norm1, w_in, sinks, w_out,
           norm2, w_up, w_down, norm_f):
    batch, seq, d = x_prompt.shape
    dec_batch, dec_seq, _ = x_sample.shape
    depth = norm1.shape[0]
    win_cache = cache_k.shape[2]
    past_len = PAST_LEN
    tm_prompt, tm_sample, tq, ff_block = 512, 512, 256, 1024
    assert dec_seq == CHUNK and win_cache == WINDOW and seq % tm_prompt == 0 and seq % tq == 0
    assert (dec_batch * dec_seq) % tm_sample == 0 and tm_sample % dec_seq == 0

    tabs_p = _all_rotary_tables(jnp.arange(seq))
    tabs_s = tuple(jnp.tile(t, (tm_sample // dec_seq, 1))
                   for t in _all_rotary_tables(past_len + jnp.arange(dec_seq)))

    hp = x_prompt.reshape(batch * seq, d)
    hs = x_sample.reshape(dec_batch * dec_seq, d)
    outs = {name: [] for name in ("kp", "vp", "sp", "ks", "vs", "ss")}
    for l in range(depth):
        qa_w, ka_w, va_w, qr_w, kr_w, vr_w, gr_w = _split_in_proj(w_in[l])
        w_cat = jnp.concatenate([_permute_heads(qa_w, 1), ka_w, qr_w, kr_w, va_w, vr_w, gr_w],
                                axis=1).astype(BF16)
        w_out_l = jnp.concatenate([_permute_heads(w_out[l, :ATTN_WIDTH], 0), w_out[l, ATTN_WIDTH:]],
                                  axis=0).astype(BF16)
        w_up_l, w_down_l = w_up[l].astype(BF16), w_down[l].astype(BF16)
        g1, g2, gf = norm1[l][None, :], norm2[l][None, :], norm_f[None, :]
        sink_rows = jnp.repeat(sinks[l].reshape(N_KV_HEADS, GROUP), CHUNK, axis=1)[:, :, None]
        final = l == depth - 1

        streams = _in_proj(hp, g1, w_cat, tabs_p, tm_prompt, seq // tm_prompt)
        mixed, state_p = _mixer_prompt(streams, sink_rows, batch, seq, tq)
        hp = _out_mlp(hp, mixed, w_out_l, g2, w_up_l, w_down_l, gf, tm_prompt, ff_block, final)
        outs["kp"].append(streams[1].reshape(batch, seq, N_KV_HEADS, HEAD_DIM)[:, -win_cache:])
        outs["vp"].append(streams[2].reshape(batch, seq, N_KV_HEADS, HEAD_DIM)[:, -win_cache:])
        outs["sp"].append(state_p)

        streams = _in_proj(hs, g1, w_cat, tabs_s, tm_sample, 1)
        ck = cache_k[l].reshape(dec_batch, win_cache, KV_WIDTH)
        cv = cache_v[l].reshape(dec_batch, win_cache, KV_WIDTH)
        mixed, state_s = _mixer_sample(streams, sink_rows, ck, cv, state_ret[l], dec_batch, dec_seq)
        hs = _out_mlp(hs, mixed, w_out_l, g2, w_up_l, w_down_l, gf, tm_sample, ff_block, final)
        k_all = jnp.concatenate([ck, streams[1].reshape(dec_batch, dec_seq, KV_WIDTH)], axis=1)
        v_all = jnp.concatenate([cv, streams[2].reshape(dec_batch, dec_seq, KV_WIDTH)], axis=1)
        outs["ks"].append(k_all[:, -win_cache:].reshape(dec_batch, win_cache, N_KV_HEADS, HEAD_DIM))
        outs["vs"].append(v_all[:, -win_cache:].reshape(dec_batch, win_cache, N_KV_HEADS, HEAD_DIM))
        outs["ss"].append(state_s)

    return (hp.reshape(batch, seq, d), hs.reshape(dec_batch, dec_seq, d),
            jnp.stack(outs["kp"], 0), jnp.stack(outs["vp"], 0), jnp.stack(outs["sp"], 0),
            jnp.stack(outs["ks"], 0), jnp.stack(outs["vs"], 0), jnp.stack(outs["ss"], 0))
```

```python
import functools

import jax
import jax.numpy as jnp
import numpy as np
from jax import lax
from jax.experimental import pallas as pl
from jax.experimental.pallas import tpu as pltpu

CHUNK = 64
HEAD_DIM = 64
N_HEADS = 8
N_KV_HEADS = 2
GROUP = N_HEADS // N_KV_HEADS
ATTN_WIDTH = N_HEADS * HEAD_DIM
KV_WIDTH = N_KV_HEADS * HEAD_DIM
WINDOW = 128
PAST_LEN = 2048
ROPE_THETA = 10000.0
RET_HEADS = 4
RET_HEAD_DIM = 128
RET_WIDTH = RET_HEADS * RET_HEAD_DIM
RET_THETA = 10000.0
EPS = 1e-6

LANES = 128
VMEM_LIMIT_BYTES = 56 * 1024 * 1024

SEC_A = ATTN_WIDTH + KV_WIDTH
SEC_R = 2 * RET_WIDTH
SEC_P = KV_WIDTH + 2 * RET_WIDTH

BF16 = jnp.bfloat16
F32 = jnp.float32


def _rms_scale(x, gain):
    return x * lax.rsqrt(jnp.mean(x * x, axis=-1, keepdims=True) + EPS) * gain


def _in_proj_kernel(x_ref, gain_ref, w_ref, cosa_ref, sina_ref, cosr_ref, sinr_ref,
                    qa_ref, k32_ref, v32_ref, k0_ref, k1_ref, v0_ref, v1_ref,
                    qr_ref, kr_ref, vr_ref, gr_ref):
    tm = x_ref.shape[0]
    a = _rms_scale(x_ref[...], gain_ref[...]).astype(BF16)
    lane = lax.broadcasted_iota(jnp.int32, (tm, LANES), 1)
    kv0_lanes = lane < HEAD_DIM

    za = jnp.dot(a, w_ref[:, 0:SEC_A], preferred_element_type=F32)
    first_half = (lane % HEAD_DIM) < (HEAD_DIM // 2)
    cosa, sina = cosa_ref[...], sina_ref[...]
    for j in range(SEC_A // LANES):
        xb = za[:, j * LANES:(j + 1) * LANES]
        partner = jnp.where(first_half, pltpu.roll(xb, LANES - HEAD_DIM // 2, 1),
                            pltpu.roll(xb, HEAD_DIM // 2, 1))
        rot = xb * cosa + partner * sina
        if j < ATTN_WIDTH // LANES:
            qa_ref[:, j * LANES:(j + 1) * LANES] = (rot * (HEAD_DIM ** -0.5)).astype(BF16)
        else:
            k32_ref[...] = rot
            k0_ref[...] = jnp.where(kv0_lanes, rot, 0.0).astype(BF16)
            k1_ref[...] = jnp.where(kv0_lanes, 0.0, rot).astype(BF16)

    zr = jnp.dot(a, w_ref[:, SEC_A:SEC_A + SEC_R], preferred_element_type=F32)
    cosr, sinr = cosr_ref[...], sinr_ref[...]
    for j in range(SEC_R // LANES):
        xb = zr[:, j * LANES:(j + 1) * LANES]
        rot = xb * cosr + pltpu.roll(xb, RET_HEAD_DIM // 2, 1) * sinr
        if j < RET_HEADS:
            qr_ref[:, j * LANES:(j + 1) * LANES] = rot.astype(BF16)
        else:
            jj = j - RET_HEADS
            kr_ref[:, jj * LANES:(jj + 1) * LANES] = (rot * (RET_HEAD_DIM ** -0.5)).astype(BF16)

    zp = jnp.dot(a, w_ref[:, SEC_A + SEC_R:], preferred_element_type=F32)
    v = zp[:, 0:KV_WIDTH]
    v32_ref[...] = v
    v0_ref[...] = jnp.where(kv0_lanes, v, 0.0).astype(BF16)
    v1_ref[...] = jnp.where(kv0_lanes, 0.0, v).astype(BF16)
    vr_ref[...] = zp[:, KV_WIDTH:KV_WIDTH + RET_WIDTH].astype(BF16)
    gr_ref[...] = zp[:, KV_WIDTH + RET_WIDTH:].astype(BF16)


def _in_proj(x2d, gain, w_cat, tables, tm, table_blocks):
    n, d = x2d.shape
    row = lambda i: (i, 0)
    const = lambda i: (0, 0)
    tab = lambda i: (i % table_blocks, 0)
    wide = lambda dt: jax.ShapeDtypeStruct((n, ATTN_WIDTH), dt)
    narrow = lambda dt: jax.ShapeDtypeStruct((n, KV_WIDTH), dt)
    out_shape = [wide(BF16), narrow(F32), narrow(F32), narrow(BF16), narrow(BF16), narrow(BF16),
                 narrow(BF16), wide(BF16), wide(BF16), wide(BF16), wide(BF16)]
    out_specs = [pl.BlockSpec((tm, s.shape[1]), row) for s in out_shape]
    in_specs = [pl.BlockSpec((tm, d), row), pl.BlockSpec((1, d), const),
                pl.BlockSpec(w_cat.shape, const)] + [pl.BlockSpec((tm, LANES), tab)] * 4
    return pl.pallas_call(
        _in_proj_kernel, grid=(n // tm,), in_specs=in_specs, out_specs=out_specs, out_shape=out_shape,
        compiler_params=pltpu.CompilerParams(dimension_semantics=("parallel",),
                                             vmem_limit_bytes=VMEM_LIMIT_BYTES),
        name="in_proj")(x2d, gain, w_cat, *tables)


def _attend_chunk(q, k0, k1, v0, v1, sink_ref, valid):
    q4 = jnp.concatenate([q[:, g * LANES:(g + 1) * LANES] for g in range(GROUP)], axis=0)
    out = None
    for kv, (k, v) in enumerate(((k0, v0), (k1, v1))):
        s = lax.dot_general(q4, k, (((1,), (1,)), ((), ())), preferred_element_type=F32)
        if valid is not None:
            s = jnp.where(valid, s, -jnp.inf)
        sink = sink_ref[kv]
        m = jnp.maximum(jnp.max(s, axis=-1, keepdims=True), sink)
        p = jnp.exp(s - m)
        denom = jnp.sum(p, axis=-1, keepdims=True) + jnp.exp(sink - m)
        o = jnp.dot(p.astype(BF16), v, preferred_element_type=F32) * (1.0 / denom)
        out = o if out is None else out + o
    return out


def _store_attn(mixed_ref, row0, o):
    for g in range(GROUP):
        mixed_ref[row0:row0 + CHUNK, g * LANES:(g + 1) * LANES] = (
            o[g * CHUNK:(g + 1) * CHUNK, :].astype(mixed_ref.dtype))


def _retention_block(qr_ref, kr_ref, vr_ref, gr_ref, dmask_ref, qdec_ref, kdec_ref, gl_ref,
                     get_state, set_state, mixed_ref):
    for h in range(RET_HEADS):
        cols = slice(h * RET_HEAD_DIM, (h + 1) * RET_HEAD_DIM)
        qh, kh, vh = qr_ref[:, cols], kr_ref[:, cols], vr_ref[:, cols]
        sc = lax.dot_general(qh, kh, (((1,), (1,)), ((), ())), preferred_element_type=F32)
        sc = sc * dmask_ref[h]
        intra = jnp.dot(sc.astype(BF16), vh, preferred_element_type=F32)
        state = get_state(h)
        cross = jnp.dot(qh, state.astype(BF16), preferred_element_type=F32) * qdec_ref[:, cols]
        o = intra + cross
        kd = (kh.astype(F32) * kdec_ref[:, cols]).astype(BF16)
        kv = lax.dot_general(kd, vh, (((0,), (0,)), ((), ())), preferred_element_type=F32)
        set_state(h, gl_ref[h] * state + kv)
        r = o * lax.rsqrt(jnp.mean(o * o, axis=-1, keepdims=True) + EPS)
        g = gr_ref[:, cols].astype(F32)
        gate = g / (1.0 + jnp.exp(-g))
        mixed_ref[:, ATTN_WIDTH + h * RET_HEAD_DIM:ATTN_WIDTH + (h + 1) * RET_HEAD_DIM] = (
            (r * gate).astype(mixed_ref.dtype))


def _mixer_prompt_kernel(qa_ref, k0_ref, k1_ref, v0_ref, v1_ref, sink_ref,
                         qr_ref, kr_ref, vr_ref, gr_ref, dmask_ref, qdec_ref, kdec_ref, gl_ref,
                         mixed_ref, state_out_ref, state_scr):
    j = pl.program_id(1)
    tq = qa_ref.shape[0]
    chunks = tq // CHUNK
    span = WINDOW + CHUNK
    past = WINDOW // CHUNK

    @pl.when(j == 0)
    def _():
        state_scr[...] = jnp.zeros_like(state_scr)

    key_idx = lax.broadcasted_iota(jnp.int32, (GROUP * CHUNK, span), 1)
    for c in range(chunks):
        cg = j * chunks + c
        start = pl.multiple_of(jnp.maximum(cg - past, 0) * CHUNK, CHUNK)
        n_valid = jnp.minimum(cg + 1, past + 1) * CHUNK
        win = pl.ds(start, span)
        o = _attend_chunk(qa_ref[c * CHUNK:(c + 1) * CHUNK, :], k0_ref[win, :], k1_ref[win, :],
                          v0_ref[win, :], v1_ref[win, :], sink_ref, key_idx < n_valid)
        _store_attn(mixed_ref, c * CHUNK, o)

    def set_state(h, val):
        state_scr[h] = val

    _retention_block(qr_ref, kr_ref, vr_ref, gr_ref, dmask_ref, qdec_ref, kdec_ref, gl_ref,
                     lambda h: state_scr[h], set_state, mixed_ref)

    @pl.when(j == pl.num_programs(1) - 1)
    def _():
        state_out_ref[0] = state_scr[...]


def _mixer_sample_kernel(qa_ref, knew_ref, vnew_ref, kcache_ref, vcache_ref, state_in_ref, sink_ref,
                         qr_ref, kr_ref, vr_ref, gr_ref, dmask_ref, qdec_ref, kdec_ref, gl_ref,
                         mixed_ref, state_out_ref):
    k = jnp.concatenate([kcache_ref[0], knew_ref[...]], axis=0)
    v = jnp.concatenate([vcache_ref[0], vnew_ref[...]], axis=0)
    kv0_lanes = lax.broadcasted_iota(jnp.int32, k.shape, 1) < HEAD_DIM
    k0 = jnp.where(kv0_lanes, k, 0.0).astype(BF16)
    k1 = jnp.where(kv0_lanes, 0.0, k).astype(BF16)
    v0 = jnp.where(kv0_lanes, v, 0.0).astype(BF16)
    v1 = jnp.where(kv0_lanes, 0.0, v).astype(BF16)
    o = _attend_chunk(qa_ref[...], k0, k1, v0, v1, sink_ref, None)
    _store_attn(mixed_ref, 0, o)

    def set_state(h, val):
        state_out_ref[0, h] = val

    _retention_block(qr_ref, kr_ref, vr_ref, gr_ref, dmask_ref, qdec_ref, kdec_ref, gl_ref,
                     lambda h: state_in_ref[0, h], set_state, mixed_ref)


def _retention_tables(length):
    log_g = np.log1p(-np.exp2(-5.0 - np.arange(RET_HEADS, dtype=np.float64)))
    idx = np.arange(length, dtype=np.float64)
    diff = idx[:, None] - idx[None, :]
    dmask = np.where(diff >= 0, np.exp(log_g[:, None, None] * np.maximum(diff, 0.0)), 0.0)
    qdec = np.exp(log_g[None, :] * (idx[:, None] + 1.0))
    kdec = np.exp(log_g[None, :] * (length - 1.0 - idx[:, None]))
    gl = np.exp(log_g * length)
    tables = (dmask, np.repeat(qdec, RET_HEAD_DIM, axis=1), np.repeat(kdec, RET_HEAD_DIM, axis=1),
              np.broadcast_to(gl[:, None, None], (RET_HEADS, 1, RET_HEAD_DIM)))
    return tuple(jnp.asarray(t, dtype=F32) for t in tables)


def _const_spec(arr):
    nd = arr.ndim
    return pl.BlockSpec(arr.shape, lambda *_: (0,) * nd)


def _mixer_prompt(streams, sink_rows, batch, seq, tq):
    qa, _, _, k0, k1, v0, v1, qr, kr, vr, gr = streams
    nt = seq // tq
    tables = _retention_tables(tq)
    tile = lambda b, j: (b * nt + j, 0)
    per_batch = lambda b, j: (b, 0)
    wide = pl.BlockSpec((tq, ATTN_WIDTH), tile)
    kvspec = pl.BlockSpec((seq, KV_WIDTH), per_batch)
    in_specs = ([wide] + [kvspec] * 4 + [_const_spec(sink_rows)] + [wide] * 4
                + [_const_spec(t) for t in tables])
    mixed, state = pl.pallas_call(
        _mixer_prompt_kernel, grid=(batch, nt), in_specs=in_specs,
        out_specs=[pl.BlockSpec((tq, ATTN_WIDTH + RET_WIDTH), tile),
                   pl.BlockSpec((1, RET_HEADS, RET_HEAD_DIM, RET_HEAD_DIM), lambda b, j: (b, 0, 0, 0))],
        out_shape=[jax.ShapeDtypeStruct((batch * seq, ATTN_WIDTH + RET_WIDTH), BF16),
                   jax.ShapeDtypeStruct((batch, RET_HEADS, RET_HEAD_DIM, RET_HEAD_DIM), F32)],
        scratch_shapes=[pltpu.VMEM((RET_HEADS, RET_HEAD_DIM, RET_HEAD_DIM), F32)],
        compiler_params=pltpu.CompilerParams(dimension_semantics=("parallel", "arbitrary"),
                                             vmem_limit_bytes=VMEM_LIMIT_BYTES),
        name="mixer_prompt")(qa, k0, k1, v0, v1, sink_rows, qr, kr, vr, gr, *tables)
    return mixed, state


def _mixer_sample(streams, sink_rows, cache_k, cache_v, state_in, batch, length):
    qa, k32, v32, _, _, _, _, qr, kr, vr, gr = streams
    tables = _retention_tables(length)
    tile = lambda b: (b, 0)
    wide = pl.BlockSpec((length, ATTN_WIDTH), tile)
    new_kv = pl.BlockSpec((length, KV_WIDTH), tile)
    cache = pl.BlockSpec((1,) + cache_k.shape[1:], lambda b: (b, 0, 0))
    state_spec = pl.BlockSpec((1, RET_HEADS, RET_HEAD_DIM, RET_HEAD_DIM), lambda b: (b, 0, 0, 0))
    in_specs = ([wide, new_kv, new_kv, cache, cache, state_spec, _const_spec(sink_rows)] + [wide] * 4
                + [_const_spec(t) for t in tables])
    mixed, state = pl.pallas_call(
        _mixer_sample_kernel, grid=(batch,), in_specs=in_specs,
        out_specs=[pl.BlockSpec((length, ATTN_WIDTH + RET_WIDTH), tile), state_spec],
        out_shape=[jax.ShapeDtypeStruct((batch * length, ATTN_WIDTH + RET_WIDTH), BF16),
                   jax.ShapeDtypeStruct(state_in.shape, F32)],
        compiler_params=pltpu.CompilerParams(dimension_semantics=("parallel",),
                                             vmem_limit_bytes=VMEM_LIMIT_BYTES),
        name="mixer_sample")(qa, k32, v32, cache_k, cache_v, state_in, sink_rows, qr, kr, vr, gr, *tables)
    return mixed, state


def _out_mlp_kernel(x_ref, mixed_ref, wout_ref, g2_ref, wup_ref, wdown_ref, gf_ref, y_ref, *,
                    ff_block, final_norm):
    h = x_ref[...] + jnp.dot(mixed_ref[...], wout_ref[...], preferred_element_type=F32)
    a = _rms_scale(h, g2_ref[...]).astype(BF16)
    acc = h
    for c in range(wup_ref.shape[1] // ff_block):
        cols = slice(c * ff_block, (c + 1) * ff_block)
        u = jnp.maximum(jnp.dot(a, wup_ref[:, cols], preferred_element_type=F32), 0.0)
        acc = acc + jnp.dot((u * u).astype(BF16), wdown_ref[cols, :], preferred_element_type=F32)
    y_ref[...] = _rms_scale(acc, gf_ref[...]) if final_norm else acc


def _out_mlp(x2d, mixed, w_out, g2, w_up, w_down, gf, tm, ff_block, final_norm):
    n, d = x2d.shape
    row = lambda i: (i, 0)
    const = lambda i: (0, 0)
    resident = lambda arr: pl.BlockSpec(arr.shape, const, pipeline_mode=pl.Buffered(1))
    in_specs = [pl.BlockSpec((tm, d), row), pl.BlockSpec((tm, mixed.shape[1]), row), resident(w_out),
                pl.BlockSpec((1, d), const), resident(w_up), resident(w_down), pl.BlockSpec((1, d), const)]
    return pl.pallas_call(
        functools.partial(_out_mlp_kernel, ff_block=ff_block, final_norm=final_norm),
        grid=(n // tm,), in_specs=in_specs, out_specs=pl.BlockSpec((tm, d), row),
        out_shape=jax.ShapeDtypeStruct((n, d), F32),
        compiler_params=pltpu.CompilerParams(dimension_semantics=("parallel",),
                                             vmem_limit_bytes=VMEM_LIMIT_BYTES),
        name="out_mlp")(x2d, mixed, w_out, g2, w_up, w_down, gf)


def _rotary_tables(pos, head_dim, theta_exponents):
    ang = pos.astype(np.float64)[:, None] / theta_exponents[None, :]
    cos, sin = np.cos(ang), np.sin(ang)
    reps = LANES // head_dim
    return (np.tile(np.concatenate([cos, cos], axis=-1), (1, reps)),
            np.tile(np.concatenate([-sin, sin], axis=-1), (1, reps)))


def _all_rotary_tables(pos, row_reps=1):
    attn = _rotary_tables(pos, HEAD_DIM, ROPE_THETA ** (np.arange(0, HEAD_DIM, 2, dtype=np.float64) / HEAD_DIM))
    ret = _rotary_tables(pos, RET_HEAD_DIM, RET_THETA ** np.linspace(0.0, 1.0, RET_HEAD_DIM // 2))
    return tuple(jnp.asarray(np.tile(t, (row_reps, 1)), dtype=F32) for t in attn + ret)


def _permute_heads(w, axis):
    shape = w.shape
    split = shape[:axis] + (N_KV_HEADS, GROUP, HEAD_DIM) + shape[axis + 1:]
    return jnp.swapaxes(w.reshape(split), axis, axis + 1).reshape(shape)


def _split_in_proj(w_in):
    bounds = [0, ATTN_WIDTH, ATTN_WIDTH + KV_WIDTH, ATTN_WIDTH + 2 * KV_WIDTH]
    bounds += [bounds[-1] + RET_WIDTH * i for i in range(1, 5)]
    return [w_in[:, lo:hi] for lo, hi in zip(bounds[:-1], bounds[1:])]


def kernel(x_prompt, x_sample, cache_k, cache_v, state_ret, norm1, w_in, sinks, w_out,
           norm2, w_up, w_down, norm_f):
    batch, seq, d = x_prompt.shape
    dec_batch, dec_seq, _ = x_sample.shape
    depth = norm1.shape[0]
    win_cache = cache_k.shape[2]
    past_len = PAST_LEN
    tm_prompt, tm_sample, tq, ff_block = 512, 512, 256, 1024
    assert dec_seq == CHUNK and win_cache == WINDOW and seq % tm_prompt == 0 and seq % tq == 0
    assert (dec_batch * dec_seq) % tm_sample == 0 and tm_sample % dec_seq == 0

    tabs_p = _all_rotary_tables(np.arange(seq))
    tabs_s = _all_rotary_tables(past_len + np.arange(dec_seq), row_reps=tm_sample // dec_seq)

    hp = x_prompt.reshape(batch * seq, d)
    hs = x_sample.reshape(dec_batch * dec_seq, d)
    outs = {name: [] for name in ("kp", "vp", "sp", "ks", "vs", "ss")}
    for l in range(depth):
        qa_w, ka_w, va_w, qr_w, kr_w, vr_w, gr_w = _split_in_proj(w_in[l])
        w_cat = jnp.concatenate([_permute_heads(qa_w, 1), ka_w, qr_w, kr_w, va_w, vr_w, gr_w],
                                axis=1).astype(BF16)
        w_out_l = jnp.concatenate([_permute_heads(w_out[l, :ATTN_WIDTH], 0), w_out[l, ATTN_WIDTH:]],
                                  axis=0).astype(BF16)
        w_up_l, w_down_l = w_up[l].astype(BF16), w_down[l].astype(BF16)
        g1, g2, gf = norm1[l][None, :], norm2[l][None, :], norm_f[None, :]
        sink_rows = jnp.repeat(sinks[l].reshape(N_KV_HEADS, GROUP), CHUNK, axis=1)[:, :, None]
        final = l == depth - 1

        streams = _in_proj(hp, g1, w_cat, tabs_p, tm_prompt, seq // tm_prompt)
        mixed, state_p = _mixer_prompt(streams, sink_rows, batch, seq, tq)
        hp = _out_mlp(hp, mixed, w_out_l, g2, w_up_l, w_down_l, gf, tm_prompt, ff_block, final)
        outs["kp"].append(streams[1].reshape(batch, seq, N_KV_HEADS, HEAD_DIM)[:, -win_cache:])
        outs["vp"].append(streams[2].reshape(batch, seq, N_KV_HEADS, HEAD_DIM)[:, -win_cache:])
        outs["sp"].append(state_p)

        streams = _in_proj(hs, g1, w_cat, tabs_s, tm_sample, 1)
        ck = cache_k[l].reshape(dec_batch, win_cache, KV_WIDTH)
        cv = cache_v[l].reshape(dec_batch, win_cache, KV_WIDTH)
        mixed, state_s = _mixer_sample(streams, sink_rows, ck, cv, state_ret[l], dec_batch, dec_seq)
        hs = _out_mlp(hs, mixed, w_out_l, g2, w_up_l, w_down_l, gf, tm_sample, ff_block, final)
        k_all = jnp.concatenate([ck, streams[1].reshape(dec_batch, dec_seq, KV_WIDTH)], axis=1)
        v_all = jnp.concatenate([cv, streams[2].reshape(dec_batch, dec_seq, KV_WIDTH)], axis=1)
        outs["ks"].append(k_all[:, -win_cache:].reshape(dec_batch, win_cache, N_KV_HEADS, HEAD_DIM))
        outs["vs"].append(v_all[:, -win_cache:].reshape(dec_batch, win_cache, N_KV_HEADS, HEAD_DIM))
        outs["ss"].append(state_s)

    return (hp.reshape(batch, seq, d), hs.reshape(dec_batch, dec_seq, d),
            jnp.stack(outs["kp"], 0), jnp.stack(outs["vp"], 0), jnp.stack(outs["sp"], 0),
            jnp.stack(outs["ks"], 0), jnp.stack(outs["vs"], 0), jnp.stack(outs["ss"], 0))
```

```python
import functools

import jax
import jax.numpy as jnp
import numpy as np
from jax import lax
from jax.experimental import pallas as pl
from jax.experimental.pallas import tpu as pltpu

CHUNK = 64
HEAD_DIM = 64
N_HEADS = 8
N_KV_HEADS = 2
GROUP = N_HEADS // N_KV_HEADS
ATTN_WIDTH = N_HEADS * HEAD_DIM
KV_WIDTH = N_KV_HEADS * HEAD_DIM
WINDOW = 128
PAST_LEN = 2048
ROPE_THETA = 10000.0
RET_HEADS = 4
RET_HEAD_DIM = 128
RET_WIDTH = RET_HEADS * RET_HEAD_DIM
RET_THETA = 10000.0
EPS = 1e-6

LANES = 128
VMEM_LIMIT_BYTES = 56 * 1024 * 1024

SEC_A = ATTN_WIDTH + KV_WIDTH
SEC_R = 2 * RET_WIDTH
SEC_P = KV_WIDTH + 2 * RET_WIDTH

BF16 = jnp.bfloat16
F32 = jnp.float32


def _rms_scale(x, gain):
    return x * lax.rsqrt(jnp.mean(x * x, axis=-1, keepdims=True) + EPS) * gain


def _in_proj_kernel(x_ref, gain_ref, w_ref, cosa_ref, sina_ref, cosr_ref, sinr_ref,
                    qa_ref, ktail_ref, vtail_ref, k0_ref, k1_ref, v0_ref, v1_ref,
                    qr_ref, kr_ref, vr_ref, gr_ref):
    tm = x_ref.shape[0]
    tail = tm - ktail_ref.shape[0]
    a = _rms_scale(x_ref[...], gain_ref[...]).astype(BF16)
    lane = lax.broadcasted_iota(jnp.int32, (tm, LANES), 1)
    kv0_lanes = lane < HEAD_DIM

    za = jnp.dot(a, w_ref[:, 0:SEC_A], preferred_element_type=F32)
    first_half = (lane % HEAD_DIM) < (HEAD_DIM // 2)
    cosa, sina = cosa_ref[...], sina_ref[...]
    for j in range(SEC_A // LANES):
        xb = za[:, j * LANES:(j + 1) * LANES]
        partner = jnp.where(first_half, pltpu.roll(xb, LANES - HEAD_DIM // 2, 1),
                            pltpu.roll(xb, HEAD_DIM // 2, 1))
        rot = xb * cosa + partner * sina
        if j < ATTN_WIDTH // LANES:
            qa_ref[:, j * LANES:(j + 1) * LANES] = (rot * (HEAD_DIM ** -0.5)).astype(BF16)
        else:
            ktail_ref[...] = rot[tail:, :]
            k0_ref[...] = jnp.where(kv0_lanes, rot, 0.0).astype(BF16)
            k1_ref[...] = jnp.where(kv0_lanes, 0.0, rot).astype(BF16)

    zr = jnp.dot(a, w_ref[:, SEC_A:SEC_A + SEC_R], preferred_element_type=F32)
    cosr, sinr = cosr_ref[...], sinr_ref[...]
    for j in range(SEC_R // LANES):
        xb = zr[:, j * LANES:(j + 1) * LANES]
        rot = xb * cosr + pltpu.roll(xb, RET_HEAD_DIM // 2, 1) * sinr
        if j < RET_HEADS:
            qr_ref[:, j * LANES:(j + 1) * LANES] = rot.astype(BF16)
        else:
            jj = j - RET_HEADS
            kr_ref[:, jj * LANES:(jj + 1) * LANES] = (rot * (RET_HEAD_DIM ** -0.5)).astype(BF16)

    zp = jnp.dot(a, w_ref[:, SEC_A + SEC_R:], preferred_element_type=F32)
    v = zp[:, 0:KV_WIDTH]
    vtail_ref[...] = v[tail:, :]
    v0_ref[...] = jnp.where(kv0_lanes, v, 0.0).astype(BF16)
    v1_ref[...] = jnp.where(kv0_lanes, 0.0, v).astype(BF16)
    vr_ref[...] = zp[:, KV_WIDTH:KV_WIDTH + RET_WIDTH].astype(BF16)
    gr_ref[...] = zp[:, KV_WIDTH + RET_WIDTH:].astype(BF16)


def _in_proj(x2d, gain, w_cat, tables, tm, seq_len, tail_rows):
    n, d = x2d.shape
    tiles_per_seq = max(seq_len // tm, 1)
    seqs_per_tile = max(tm // seq_len, 1)
    tail_block = seqs_per_tile * tail_rows
    assert tail_block == tail_rows or tail_rows == seq_len
    row = lambda i: (i, 0)
    const = lambda i: (0, 0)
    tab = lambda i: (i % tiles_per_seq, 0)
    tail = lambda i: (i // tiles_per_seq, 0)
    wide = lambda dt: jax.ShapeDtypeStruct((n, ATTN_WIDTH), dt)
    narrow = lambda dt: jax.ShapeDtypeStruct((n, KV_WIDTH), dt)
    tails = jax.ShapeDtypeStruct((n // seq_len * tail_rows, KV_WIDTH), F32)
    out_shape = [wide(BF16), tails, tails, narrow(BF16), narrow(BF16), narrow(BF16),
                 narrow(BF16), wide(BF16), wide(BF16), wide(BF16), wide(BF16)]
    out_specs = [pl.BlockSpec((tail_block, KV_WIDTH), tail) if s is tails
                 else pl.BlockSpec((tm, s.shape[1]), row) for s in out_shape]
    in_specs = [pl.BlockSpec((tm, d), row), pl.BlockSpec((1, d), const),
                pl.BlockSpec(w_cat.shape, const)] + [pl.BlockSpec((tm, LANES), tab)] * 4
    return pl.pallas_call(
        _in_proj_kernel, grid=(n // tm,), in_specs=in_specs, out_specs=out_specs, out_shape=out_shape,
        compiler_params=pltpu.CompilerParams(dimension_semantics=("arbitrary",),
                                             vmem_limit_bytes=VMEM_LIMIT_BYTES),
        name="in_proj")(x2d, gain, w_cat, *tables)


def _attend_chunks(chunks, sink_ref):
    scores = []
    for q, ks, _, _ in chunks:
        q4 = jnp.concatenate([q[:, g * LANES:(g + 1) * LANES] for g in range(GROUP)], axis=0)
        scores.append([lax.dot_general(q4, k, (((1,), (1,)), ((), ())), preferred_element_type=F32)
                       for k in ks])
    probs = []
    for (_, _, _, valid), s_pair in zip(chunks, scores):
        p_pair = []
        for kv, s in enumerate(s_pair):
            if valid is not None:
                s = jnp.where(valid, s, -jnp.inf)
            sink = sink_ref[kv]
            m = jnp.maximum(jnp.max(s, axis=-1, keepdims=True), sink)
            p = jnp.exp(s - m)
            denom = jnp.sum(p, axis=-1, keepdims=True) + jnp.exp(sink - m)
            p_pair.append((p.astype(BF16), 1.0 / denom))
        probs.append(p_pair)
    outs = []
    for (_, _, vs, _), p_pair in zip(chunks, probs):
        o0, o1 = (jnp.dot(p, v, preferred_element_type=F32) * inv for (p, inv), v in zip(p_pair, vs))
        outs.append(o0 + o1)
    return outs


def _store_attn(mixed_ref, row0, o):
    for g in range(GROUP):
        mixed_ref[row0:row0 + CHUNK, g * LANES:(g + 1) * LANES] = (
            o[g * CHUNK:(g + 1) * CHUNK, :].astype(mixed_ref.dtype))


def _retention_block(qr_ref, kr_ref, vr_ref, gr_ref, dmask_ref, qdec_ref, kdec_ref, gl_ref,
                     get_state, set_state, mixed_ref):
    heads = range(RET_HEADS)
    cols = [slice(h * RET_HEAD_DIM, (h + 1) * RET_HEAD_DIM) for h in heads]
    contract_last = (((1,), (1,)), ((), ()))
    contract_rows = (((0,), (0,)), ((), ()))
    states = [get_state(h) for h in heads]
    sc = [lax.dot_general(qr_ref[:, c], kr_ref[:, c], contract_last, preferred_element_type=F32)
          for c in cols]
    cross = [jnp.dot(qr_ref[:, c], st.astype(BF16), preferred_element_type=F32)
             for c, st in zip(cols, states)]
    kd = [(kr_ref[:, c].astype(F32) * kdec_ref[:, c]).astype(BF16) for c in cols]
    kv = [lax.dot_general(k, vr_ref[:, c], contract_rows, preferred_element_type=F32)
          for k, c in zip(kd, cols)]
    intra = [jnp.dot((s * dmask_ref[h]).astype(BF16), vr_ref[:, c], preferred_element_type=F32)
             for h, (s, c) in enumerate(zip(sc, cols))]
    for h in heads:
        set_state(h, gl_ref[h] * states[h] + kv[h])
        o = intra[h] + cross[h] * qdec_ref[:, cols[h]]
        r = o * lax.rsqrt(jnp.mean(o * o, axis=-1, keepdims=True) + EPS)
        g = gr_ref[:, cols[h]].astype(F32)
        gate = g / (1.0 + jnp.exp(-g))
        mixed_ref[:, ATTN_WIDTH + h * RET_HEAD_DIM:ATTN_WIDTH + (h + 1) * RET_HEAD_DIM] = (
            (r * gate).astype(mixed_ref.dtype))


def _mixer_prompt_kernel(qa_ref, k0_ref, k1_ref, v0_ref, v1_ref, sink_ref,
                         qr_ref, kr_ref, vr_ref, gr_ref, dmask_ref, qdec_ref, kdec_ref, gl_ref,
                         mixed_ref, state_out_ref, state_scr):
    j = pl.program_id(1)
    tq = qa_ref.shape[0]
    chunks = tq // CHUNK
    span = WINDOW + CHUNK
    past = WINDOW // CHUNK

    @pl.when(j == 0)
    def _():
        state_scr[...] = jnp.zeros_like(state_scr)

    key_idx = lax.broadcasted_iota(jnp.int32, (GROUP * CHUNK, span), 1)
    work = []
    for c in range(chunks):
        cg = j * chunks + c
        start = pl.multiple_of(jnp.maximum(cg - past, 0) * CHUNK, CHUNK)
        n_valid = jnp.minimum(cg + 1, past + 1) * CHUNK
        win = pl.ds(start, span)
        work.append((qa_ref[c * CHUNK:(c + 1) * CHUNK, :], (k0_ref[win, :], k1_ref[win, :]),
                     (v0_ref[win, :], v1_ref[win, :]), key_idx < n_valid))
    for c, o in enumerate(_attend_chunks(work, sink_ref)):
        _store_attn(mixed_ref, c * CHUNK, o)

    def set_state(h, val):
        state_scr[h] = val

    _retention_block(qr_ref, kr_ref, vr_ref, gr_ref, dmask_ref, qdec_ref, kdec_ref, gl_ref,
                     lambda h: state_scr[h], set_state, mixed_ref)

    @pl.when(j == pl.num_programs(1) - 1)
    def _():
        state_out_ref[0] = state_scr[...]


def _mixer_sample_kernel(qa_ref, knew_ref, vnew_ref, kcache_ref, vcache_ref, state_in_ref, sink_ref,
                         qr_ref, kr_ref, vr_ref, gr_ref, dmask_ref, qdec_ref, kdec_ref, gl_ref,
                         mixed_ref, state_out_ref):
    k = jnp.concatenate([kcache_ref[0], knew_ref[...]], axis=0)
    v = jnp.concatenate([vcache_ref[0], vnew_ref[...]], axis=0)
    kv0_lanes = lax.broadcasted_iota(jnp.int32, k.shape, 1) < HEAD_DIM
    k0 = jnp.where(kv0_lanes, k, 0.0).astype(BF16)
    k1 = jnp.where(kv0_lanes, 0.0, k).astype(BF16)
    v0 = jnp.where(kv0_lanes, v, 0.0).astype(BF16)
    v1 = jnp.where(kv0_lanes, 0.0, v).astype(BF16)
    (o,) = _attend_chunks([(qa_ref[...], (k0, k1), (v0, v1), None)], sink_ref)
    _store_attn(mixed_ref, 0, o)

    def set_state(h, val):
        state_out_ref[0, h] = val

    _retention_block(qr_ref, kr_ref, vr_ref, gr_ref, dmask_ref, qdec_ref, kdec_ref, gl_ref,
                     lambda h: state_in_ref[0, h], set_state, mixed_ref)


def _retention_tables(length):
    log_g = np.log1p(-np.exp2(-5.0 - np.arange(RET_HEADS, dtype=np.float64)))
    idx = np.arange(length, dtype=np.float64)
    diff = idx[:, None] - idx[None, :]
    dmask = np.where(diff >= 0, np.exp(log_g[:, None, None] * np.maximum(diff, 0.0)), 0.0)
    qdec = np.exp(log_g[None, :] * (idx[:, None] + 1.0))
    kdec = np.exp(log_g[None, :] * (length - 1.0 - idx[:, None]))
    gl = np.exp(log_g * length)
    tables = (dmask, np.repeat(qdec, RET_HEAD_DIM, axis=1), np.repeat(kdec, RET_HEAD_DIM, axis=1),
              np.broadcast_to(gl[:, None, None], (RET_HEADS, 1, RET_HEAD_DIM)))
    return tuple(jnp.asarray(t, dtype=F32) for t in tables)


def _const_spec(arr):
    nd = arr.ndim
    return pl.BlockSpec(arr.shape, lambda *_: (0,) * nd)


def _mixer_prompt(streams, sink_rows, batch, seq, tq):
    qa, _, _, k0, k1, v0, v1, qr, kr, vr, gr = streams
    nt = seq // tq
    tables = _retention_tables(tq)
    tile = lambda b, j: (b * nt + j, 0)
    per_batch = lambda b, j: (b, 0)
    wide = pl.BlockSpec((tq, ATTN_WIDTH), tile)
    kvspec = pl.BlockSpec((seq, KV_WIDTH), per_batch)
    in_specs = ([wide] + [kvspec] * 4 + [_const_spec(sink_rows)] + [wide] * 4
                + [_const_spec(t) for t in tables])
    mixed, state = pl.pallas_call(
        _mixer_prompt_kernel, grid=(batch, nt), in_specs=in_specs,
        out_specs=[pl.BlockSpec((tq, ATTN_WIDTH + RET_WIDTH), tile),
                   pl.BlockSpec((1, RET_HEADS, RET_HEAD_DIM, RET_HEAD_DIM), lambda b, j: (b, 0, 0, 0))],
        out_shape=[jax.ShapeDtypeStruct((batch * seq, ATTN_WIDTH + RET_WIDTH), BF16),
                   jax.ShapeDtypeStruct((batch, RET_HEADS, RET_HEAD_DIM, RET_HEAD_DIM), F32)],
        scratch_shapes=[pltpu.VMEM((RET_HEADS, RET_HEAD_DIM, RET_HEAD_DIM), F32)],
        compiler_params=pltpu.CompilerParams(dimension_semantics=("parallel", "arbitrary"),
                                             vmem_limit_bytes=VMEM_LIMIT_BYTES),
        name="mixer_prompt")(qa, k0, k1, v0, v1, sink_rows, qr, kr, vr, gr, *tables)
    return mixed, state


def _mixer_sample(streams, sink_rows, cache_k, cache_v, state_in, batch, length):
    qa, k32, v32, _, _, _, _, qr, kr, vr, gr = streams
    tables = _retention_tables(length)
    tile = lambda b: (b, 0)
    wide = pl.BlockSpec((length, ATTN_WIDTH), tile)
    new_kv = pl.BlockSpec((length, KV_WIDTH), tile)
    cache = pl.BlockSpec((1,) + cache_k.shape[1:], lambda b: (b, 0, 0))
    state_spec = pl.BlockSpec((1, RET_HEADS, RET_HEAD_DIM, RET_HEAD_DIM), lambda b: (b, 0, 0, 0))
    in_specs = ([wide, new_kv, new_kv, cache, cache, state_spec, _const_spec(sink_rows)] + [wide] * 4
                + [_const_spec(t) for t in tables])
    mixed, state = pl.pallas_call(
        _mixer_sample_kernel, grid=(batch,), in_specs=in_specs,
        out_specs=[pl.BlockSpec((length, ATTN_WIDTH + RET_WIDTH), tile), state_spec],
        out_shape=[jax.ShapeDtypeStruct((batch * length, ATTN_WIDTH + RET_WIDTH), BF16),
                   jax.ShapeDtypeStruct(state_in.shape, F32)],
        compiler_params=pltpu.CompilerParams(dimension_semantics=("parallel",),
                                             vmem_limit_bytes=VMEM_LIMIT_BYTES),
        name="mixer_sample")(qa, k32, v32, cache_k, cache_v, state_in, sink_rows, qr, kr, vr, gr, *tables)
    return mixed, state


def _out_mlp_kernel(x_ref, mixed_ref, wout_ref, g2_ref, wup_ref, wdown_ref, gf_ref, y_ref, *,
                    ff_block, final_norm):
    h = x_ref[...] + jnp.dot(mixed_ref[...], wout_ref[...], preferred_element_type=F32)
    a = _rms_scale(h, g2_ref[...]).astype(BF16)
    acc = h
    for c in range(wup_ref.shape[1] // ff_block):
        cols = slice(c * ff_block, (c + 1) * ff_block)
        u = jnp.maximum(jnp.dot(a, wup_ref[:, cols], preferred_element_type=F32), 0.0)
        acc = acc + jnp.dot((u * u).astype(BF16), wdown_ref[cols, :], preferred_element_type=F32)
    y_ref[...] = _rms_scale(acc, gf_ref[...]) if final_norm else acc


def _out_mlp(x2d, mixed, w_out, g2, w_up, w_down, gf, tm, ff_block, final_norm):
    n, d = x2d.shape
    row = lambda i: (i, 0)
    const = lambda i: (0, 0)
    resident = lambda arr: pl.BlockSpec(arr.shape, const, pipeline_mode=pl.Buffered(1))
    in_specs = [pl.BlockSpec((tm, d), row), pl.BlockSpec((tm, mixed.shape[1]), row), resident(w_out),
                pl.BlockSpec((1, d), const), resident(w_up), resident(w_down), pl.BlockSpec((1, d), const)]
    return pl.pallas_call(
        functools.partial(_out_mlp_kernel, ff_block=ff_block, final_norm=final_norm),
        grid=(n // tm,), in_specs=in_specs, out_specs=pl.BlockSpec((tm, d), row),
        out_shape=jax.ShapeDtypeStruct((n, d), F32),
        compiler_params=pltpu.CompilerParams(dimension_semantics=("parallel",),
                                             vmem_limit_bytes=VMEM_LIMIT_BYTES),
        name="out_mlp")(x2d, mixed, w_out, g2, w_up, w_down, gf)


def _rotary_tables(pos, head_dim, theta_exponents):
    ang = pos.astype(np.float64)[:, None] / theta_exponents[None, :]
    cos, sin = np.cos(ang), np.sin(ang)
    reps = LANES // head_dim
    return (np.tile(np.concatenate([cos, cos], axis=-1), (1, reps)),
            np.tile(np.concatenate([-sin, sin], axis=-1), (1, reps)))


def _all_rotary_tables(pos, row_reps=1):
    attn = _rotary_tables(pos, HEAD_DIM, ROPE_THETA ** (np.arange(0, HEAD_DIM, 2, dtype=np.float64) / HEAD_DIM))
    ret = _rotary_tables(pos, RET_HEAD_DIM, RET_THETA ** np.linspace(0.0, 1.0, RET_HEAD_DIM // 2))
    return tuple(jnp.asarray(np.tile(t, (row_reps, 1)), dtype=F32) for t in attn + ret)


def _permute_heads(w, axis):
    shape = w.shape
    split = shape[:axis] + (N_KV_HEADS, GROUP, HEAD_DIM) + shape[axis + 1:]
    return jnp.swapaxes(w.reshape(split), axis, axis + 1).reshape(shape)


def _split_in_proj(w_in):
    bounds = [0, ATTN_WIDTH, ATTN_WIDTH + KV_WIDTH, ATTN_WIDTH + 2 * KV_WIDTH]
    bounds += [bounds[-1] + RET_WIDTH * i for i in range(1, 5)]
    return [w_in[:, lo:hi] for lo, hi in zip(bounds[:-1], bounds[1:])]


def kernel(x_prompt, x_sample, cache_k, cache_v, state_ret, norm1, w_in, sinks, w_out,
           norm2, w_up, w_down, norm_f):
    batch, seq, d = x_prompt.shape
    dec_batch, dec_seq, _ = x_sample.shape
    depth = norm1.shape[0]
    win_cache = cache_k.shape[2]
    past_len = PAST_LEN
    tm_prompt, tm_sample, tq, ff_block = 512, 512, 256, 1024
    assert dec_seq == CHUNK and win_cache == WINDOW and seq % tm_prompt == 0 and seq % tq == 0
    assert (dec_batch * dec_seq) % tm_sample == 0 and tm_sample % dec_seq == 0

    tabs_p = _all_rotary_tables(np.arange(seq))
    tabs_s = _all_rotary_tables(past_len + np.arange(dec_seq), row_reps=tm_sample // dec_seq)

    hp = x_prompt.reshape(batch * seq, d)
    hs = x_sample.reshape(dec_batch * dec_seq, d)
    outs = {name: [] for name in ("kp", "vp", "sp", "ks", "vs", "ss")}
    for l in range(depth):
        qa_w, ka_w, va_w, qr_w, kr_w, vr_w, gr_w = _split_in_proj(w_in[l])
        w_cat = jnp.concatenate([_permute_heads(qa_w, 1), ka_w, qr_w, kr_w, va_w, vr_w, gr_w],
                                axis=1).astype(BF16)
        w_out_l = jnp.concatenate([_permute_heads(w_out[l, :ATTN_WIDTH], 0), w_out[l, ATTN_WIDTH:]],
                                  axis=0).astype(BF16)
        w_up_l, w_down_l = w_up[l].astype(BF16), w_down[l].astype(BF16)
        g1, g2, gf = norm1[l][None, :], norm2[l][None, :], norm_f[None, :]
        sink_rows = jnp.repeat(sinks[l].reshape(N_KV_HEADS, GROUP), CHUNK, axis=1)[:, :, None]
        final = l == depth - 1

        streams = _in_proj(hp, g1, w_cat, tabs_p, tm_prompt, seq, win_cache)
        mixed, state_p = _mixer_prompt(streams, sink_rows, batch, seq, tq)
        hp = _out_mlp(hp, mixed, w_out_l, g2, w_up_l, w_down_l, gf, tm_prompt, ff_block, final)
        outs["kp"].append(streams[1].reshape(batch, win_cache, N_KV_HEADS, HEAD_DIM))
        outs["vp"].append(streams[2].reshape(batch, win_cache, N_KV_HEADS, HEAD_DIM))
        outs["sp"].append(state_p)

        streams = _in_proj(hs, g1, w_cat, tabs_s, tm_sample, dec_seq, dec_seq)
        ck = cache_k[l].reshape(dec_batch, win_cache, KV_WIDTH)
        cv = cache_v[l].reshape(dec_batch, win_cache, KV_WIDTH)
        mixed, state_s = _mixer_sample(streams, sink_rows, ck, cv, state_ret[l], dec_batch, dec_seq)
        hs = _out_mlp(hs, mixed, w_out_l, g2, w_up_l, w_down_l, gf, tm_sample, ff_block, final)
        k_all = jnp.concatenate([ck, streams[1].reshape(dec_batch, dec_seq, KV_WIDTH)], axis=1)
        v_all = jnp.concatenate([cv, streams[2].reshape(dec_batch, dec_seq, KV_WIDTH)], axis=1)
        outs["ks"].append(k_all[:, -win_cache:].reshape(dec_batch, win_cache, N_KV_HEADS, HEAD_DIM))
        outs["vs"].append(v_all[:, -win_cache:].reshape(dec_batch, win_cache, N_KV_HEADS, HEAD_DIM))
        outs["ss"].append(state_s)

    return (hp.reshape(batch, seq, d), hs.reshape(dec_batch, dec_seq, d),
            jnp.stack(outs["kp"], 0), jnp.stack(outs["vp"], 0), jnp.stack(outs["sp"], 0),
            jnp.stack(outs["ks"], 0), jnp.stack(outs["vs"], 0), jnp.stack(outs["ss"], 0))
```

```python
import functools

import jax
import jax.numpy as jnp
import numpy as np
from jax import lax
from jax.experimental import pallas as pl
from jax.experimental.pallas import tpu as pltpu

CHUNK = 64
HEAD_DIM = 64
N_HEADS = 8
N_KV_HEADS = 2
GROUP = N_HEADS // N_KV_HEADS
ATTN_WIDTH = N_HEADS * HEAD_DIM
KV_WIDTH = N_KV_HEADS * HEAD_DIM
WINDOW = 128
PAST_LEN = 2048
ROPE_THETA = 10000.0
RET_HEADS = 4
RET_HEAD_DIM = 128
RET_WIDTH = RET_HEADS * RET_HEAD_DIM
RET_THETA = 10000.0
EPS = 1e-6

LANES = 128
VMEM_LIMIT_BYTES = 56 * 1024 * 1024

SEC_A = ATTN_WIDTH + KV_WIDTH
SEC_R = 2 * RET_WIDTH
SEC_P = KV_WIDTH + 2 * RET_WIDTH

BF16 = jnp.bfloat16
F32 = jnp.float32


def _rms_scale(x, gain):
    return x * lax.rsqrt(jnp.mean(x * x, axis=-1, keepdims=True) + EPS) * gain


def _in_proj_kernel(x_ref, gain_ref, w_ref, cosa_ref, sina_ref, cosr_ref, sinr_ref,
                    qbd_ref, ktail_ref, vtail_ref, kb_ref, vb_ref,
                    qr_ref, kr_ref, vr_ref, gr_ref):
    tm = x_ref.shape[0]
    tail = tm - ktail_ref.shape[0]
    a = _rms_scale(x_ref[...], gain_ref[...]).astype(BF16)
    lane = lax.broadcasted_iota(jnp.int32, (tm, LANES), 1)
    kv0_lanes = lane < HEAD_DIM
    n_chunks = tm // CHUNK

    za = jnp.dot(a, w_ref[:, 0:SEC_A], preferred_element_type=F32)
    first_half = (lane % HEAD_DIM) < (HEAD_DIM // 2)
    cosa, sina = cosa_ref[...], sina_ref[...]
    for j in range(SEC_A // LANES):
        xb = za[:, j * LANES:(j + 1) * LANES]
        partner = jnp.where(first_half, pltpu.roll(xb, LANES - HEAD_DIM // 2, 1),
                            pltpu.roll(xb, HEAD_DIM // 2, 1))
        rot = xb * cosa + partner * sina
        if j < ATTN_WIDTH // LANES:
            q = rot * (HEAD_DIM ** -0.5)
            for kv, qm in enumerate((jnp.where(kv0_lanes, q, 0.0), jnp.where(kv0_lanes, 0.0, q))):
                qbd_ref[:, kv * GROUP + j, :, :] = qm.astype(BF16).reshape(n_chunks, CHUNK, LANES)
        else:
            ktail_ref[...] = rot[tail:, :]
            kb_ref[...] = rot.astype(BF16)

    zr = jnp.dot(a, w_ref[:, SEC_A:SEC_A + SEC_R], preferred_element_type=F32)
    cosr, sinr = cosr_ref[...], sinr_ref[...]
    for j in range(SEC_R // LANES):
        xb = zr[:, j * LANES:(j + 1) * LANES]
        rot = xb * cosr + pltpu.roll(xb, RET_HEAD_DIM // 2, 1) * sinr
        if j < RET_HEADS:
            qr_ref[:, j * LANES:(j + 1) * LANES] = rot.astype(BF16)
        else:
            jj = j - RET_HEADS
            kr_ref[:, jj * LANES:(jj + 1) * LANES] = (rot * (RET_HEAD_DIM ** -0.5)).astype(BF16)

    zp = jnp.dot(a, w_ref[:, SEC_A + SEC_R:], preferred_element_type=F32)
    v = zp[:, 0:KV_WIDTH]
    vtail_ref[...] = v[tail:, :]
    vb_ref[...] = v.astype(BF16)
    vr_ref[...] = zp[:, KV_WIDTH:KV_WIDTH + RET_WIDTH].astype(BF16)
    gr_ref[...] = zp[:, KV_WIDTH + RET_WIDTH:].astype(BF16)


def _in_proj(x2d, gain, w_cat, tables, tm, seq_len, tail_rows):
    n, d = x2d.shape
    tiles_per_seq = max(seq_len // tm, 1)
    seqs_per_tile = max(tm // seq_len, 1)
    tail_block = seqs_per_tile * tail_rows
    assert tail_block == tail_rows or tail_rows == seq_len
    row = lambda i: (i, 0)
    const = lambda i: (0, 0)
    tab = lambda i: (i % tiles_per_seq, 0)
    tail = lambda i: (i // tiles_per_seq, 0)
    wide = lambda dt: jax.ShapeDtypeStruct((n, ATTN_WIDTH), dt)
    narrow = lambda dt: jax.ShapeDtypeStruct((n, KV_WIDTH), dt)
    tails = jax.ShapeDtypeStruct((n // seq_len * tail_rows, KV_WIDTH), F32)
    qbd = jax.ShapeDtypeStruct((n // CHUNK, N_HEADS, CHUNK, LANES), BF16)
    out_shape = [qbd, tails, tails, narrow(BF16), narrow(BF16), wide(BF16), wide(BF16), wide(BF16), wide(BF16)]
    out_specs = [pl.BlockSpec((tm // CHUNK, N_HEADS, CHUNK, LANES), lambda i: (i, 0, 0, 0)) if s is qbd
                 else pl.BlockSpec((tail_block, KV_WIDTH), tail) if s is tails
                 else pl.BlockSpec((tm, s.shape[1]), row) for s in out_shape]
    in_specs = [pl.BlockSpec((tm, d), row), pl.BlockSpec((1, d), const),
                pl.BlockSpec(w_cat.shape, const)] + [pl.BlockSpec((tm, LANES), tab)] * 4
    return pl.pallas_call(
        _in_proj_kernel, grid=(n // tm,), in_specs=in_specs, out_specs=out_specs, out_shape=out_shape,
        compiler_params=pltpu.CompilerParams(dimension_semantics=("arbitrary",),
                                             vmem_limit_bytes=VMEM_LIMIT_BYTES),
        name="in_proj")(x2d, gain, w_cat, *tables)


def _attend_chunks(chunks, sink_ref):
    sink = sink_ref[...]
    contract_last = (((1,), (1,)), ((), ()))
    contract_rows = (((0,), (0,)), ((), ()))
    scores = [lax.dot_general(k, q, contract_last, preferred_element_type=F32) for q, k, _, _ in chunks]
    probs = []
    for (_, _, _, n_valid), s in zip(chunks, scores):
        if n_valid is not None:
            s = jnp.where(lax.broadcasted_iota(jnp.int32, s.shape, 0) < n_valid, s, -jnp.inf)
        m = jnp.maximum(jnp.max(s, axis=0, keepdims=True), sink)
        p = jnp.exp(s - m)
        denom = jnp.sum(p, axis=0, keepdims=True) + jnp.exp(sink - m)
        probs.append((p.astype(BF16), 1.0 / denom))
    outs = []
    half = GROUP * CHUNK
    for (_, _, v, _), (p, inv) in zip(chunks, probs):
        ot = lax.dot_general(v, p, contract_rows, preferred_element_type=F32) * inv
        own = jnp.concatenate([ot[:HEAD_DIM, :half], ot[HEAD_DIM:, half:]], axis=0)
        outs.append(own.T)
    return outs


def _store_attn(mixed_ref, row0, o):
    for g in range(GROUP):
        mixed_ref[row0:row0 + CHUNK, g * LANES:(g + 1) * LANES] = (
            o[g * CHUNK:(g + 1) * CHUNK, :].astype(mixed_ref.dtype))


def _retention_block(qr_ref, kr_ref, vr_ref, gr_ref, dmask_ref, qdec_ref, kdec_ref, gl_ref,
                     get_state, set_state, mixed_ref):
    heads = range(RET_HEADS)
    cols = [slice(h * RET_HEAD_DIM, (h + 1) * RET_HEAD_DIM) for h in heads]
    contract_last = (((1,), (1,)), ((), ()))
    contract_rows = (((0,), (0,)), ((), ()))
    states = [get_state(h) for h in heads]
    sc = [lax.dot_general(qr_ref[:, c], kr_ref[:, c], contract_last, preferred_element_type=F32)
          for c in cols]
    cross = [jnp.dot(qr_ref[:, c], st.astype(BF16), preferred_element_type=F32)
             for c, st in zip(cols, states)]
    kd = [(kr_ref[:, c].astype(F32) * kdec_ref[:, c]).astype(BF16) for c in cols]
    kv = [lax.dot_general(k, vr_ref[:, c], contract_rows, preferred_element_type=F32)
          for k, c in zip(kd, cols)]
    intra = [jnp.dot((s * dmask_ref[h]).astype(BF16), vr_ref[:, c], preferred_element_type=F32)
             for h, (s, c) in enumerate(zip(sc, cols))]
    for h in heads:
        set_state(h, gl_ref[h] * states[h] + kv[h])
        o = intra[h] + cross[h] * qdec_ref[:, cols[h]]
        r = o * lax.rsqrt(jnp.mean(o * o, axis=-1, keepdims=True) + EPS)
        g = gr_ref[:, cols[h]].astype(F32)
        gate = g / (1.0 + jnp.exp(-g))
        mixed_ref[:, ATTN_WIDTH + h * RET_HEAD_DIM:ATTN_WIDTH + (h + 1) * RET_HEAD_DIM] = (
            (r * gate).astype(mixed_ref.dtype))


def _mixer_prompt_kernel(qbd_ref, kb_ref, vb_ref, sink_ref,
                         qr_ref, kr_ref, vr_ref, gr_ref, dmask_ref, qdec_ref, kdec_ref, gl_ref,
                         mixed_ref, state_out_ref, state_scr):
    j = pl.program_id(1)
    chunks = qbd_ref.shape[0]
    span = WINDOW + CHUNK
    past = WINDOW // CHUNK

    @pl.when(j == 0)
    def _():
        state_scr[...] = jnp.zeros_like(state_scr)

    work = []
    for c in range(chunks):
        cg = j * chunks + c
        start = pl.multiple_of(jnp.maximum(cg - past, 0) * CHUNK, CHUNK)
        n_valid = jnp.minimum(cg + 1, past + 1) * CHUNK
        win = pl.ds(start, span)
        work.append((qbd_ref[c].reshape(N_HEADS * CHUNK, LANES), kb_ref[win, :], vb_ref[win, :], n_valid))
    for c, o in enumerate(_attend_chunks(work, sink_ref)):
        _store_attn(mixed_ref, c * CHUNK, o)

    def set_state(h, val):
        state_scr[h] = val

    _retention_block(qr_ref, kr_ref, vr_ref, gr_ref, dmask_ref, qdec_ref, kdec_ref, gl_ref,
                     lambda h: state_scr[h], set_state, mixed_ref)

    @pl.when(j == pl.num_programs(1) - 1)
    def _():
        state_out_ref[0] = state_scr[...]


def _mixer_sample_kernel(qbd_ref, knew_ref, vnew_ref, kcache_ref, vcache_ref, state_in_ref, sink_ref,
                         qr_ref, kr_ref, vr_ref, gr_ref, dmask_ref, qdec_ref, kdec_ref, gl_ref,
                         mixed_ref, state_out_ref):
    k = jnp.concatenate([kcache_ref[0].astype(BF16), knew_ref[...]], axis=0)
    v = jnp.concatenate([vcache_ref[0].astype(BF16), vnew_ref[...]], axis=0)
    (o,) = _attend_chunks([(qbd_ref[0].reshape(N_HEADS * CHUNK, LANES), k, v, None)], sink_ref)
    _store_attn(mixed_ref, 0, o)

    def set_state(h, val):
        state_out_ref[0, h] = val

    _retention_block(qr_ref, kr_ref, vr_ref, gr_ref, dmask_ref, qdec_ref, kdec_ref, gl_ref,
                     lambda h: state_in_ref[0, h], set_state, mixed_ref)


def _retention_tables(length):
    log_g = np.log1p(-np.exp2(-5.0 - np.arange(RET_HEADS, dtype=np.float64)))
    idx = np.arange(length, dtype=np.float64)
    diff = idx[:, None] - idx[None, :]
    dmask = np.where(diff >= 0, np.exp(log_g[:, None, None] * np.maximum(diff, 0.0)), 0.0)
    qdec = np.exp(log_g[None, :] * (idx[:, None] + 1.0))
    kdec = np.exp(log_g[None, :] * (length - 1.0 - idx[:, None]))
    gl = np.exp(log_g * length)
    tables = (dmask, np.repeat(qdec, RET_HEAD_DIM, axis=1), np.repeat(kdec, RET_HEAD_DIM, axis=1),
              np.broadcast_to(gl[:, None, None], (RET_HEADS, 1, RET_HEAD_DIM)))
    return tuple(jnp.asarray(t, dtype=F32) for t in tables)


def _const_spec(arr):
    nd = arr.ndim
    return pl.BlockSpec(arr.shape, lambda *_: (0,) * nd)


def _mixer_prompt(streams, sink_rows, batch, seq, tq):
    qbd, _, _, kb, vb, qr, kr, vr, gr = streams
    nt = seq // tq
    tables = _retention_tables(tq)
    tile = lambda b, j: (b * nt + j, 0)
    per_batch = lambda b, j: (b, 0)
    wide = pl.BlockSpec((tq, ATTN_WIDTH), tile)
    qspec = pl.BlockSpec((tq // CHUNK, N_HEADS, CHUNK, LANES), lambda b, j: (b * nt + j, 0, 0, 0))
    kvspec = pl.BlockSpec((seq, KV_WIDTH), per_batch)
    in_specs = ([qspec] + [kvspec] * 2 + [_const_spec(sink_rows)] + [wide] * 4
                + [_const_spec(t) for t in tables])
    mixed, state = pl.pallas_call(
        _mixer_prompt_kernel, grid=(batch, nt), in_specs=in_specs,
        out_specs=[pl.BlockSpec((tq, ATTN_WIDTH + RET_WIDTH), tile),
                   pl.BlockSpec((1, RET_HEADS, RET_HEAD_DIM, RET_HEAD_DIM), lambda b, j: (b, 0, 0, 0))],
        out_shape=[jax.ShapeDtypeStruct((batch * seq, ATTN_WIDTH + RET_WIDTH), BF16),
                   jax.ShapeDtypeStruct((batch, RET_HEADS, RET_HEAD_DIM, RET_HEAD_DIM), F32)],
        scratch_shapes=[pltpu.VMEM((RET_HEADS, RET_HEAD_DIM, RET_HEAD_DIM), F32)],
        compiler_params=pltpu.CompilerParams(dimension_semantics=("parallel", "arbitrary"),
                                             vmem_limit_bytes=VMEM_LIMIT_BYTES),
        name="mixer_prompt")(qbd, kb, vb, sink_rows, qr, kr, vr, gr, *tables)
    return mixed, state


def _mixer_sample(streams, sink_rows, cache_k, cache_v, state_in, batch, length):
    qbd, _, _, kb, vb, qr, kr, vr, gr = streams
    tables = _retention_tables(length)
    tile = lambda b: (b, 0)
    wide = pl.BlockSpec((length, ATTN_WIDTH), tile)
    qspec = pl.BlockSpec((length // CHUNK, N_HEADS, CHUNK, LANES), lambda b: (b, 0, 0, 0))
    new_kv = pl.BlockSpec((length, KV_WIDTH), tile)
    cache = pl.BlockSpec((1,) + cache_k.shape[1:], lambda b: (b, 0, 0))
    state_spec = pl.BlockSpec((1, RET_HEADS, RET_HEAD_DIM, RET_HEAD_DIM), lambda b: (b, 0, 0, 0))
    in_specs = ([qspec, new_kv, new_kv, cache, cache, state_spec, _const_spec(sink_rows)] + [wide] * 4
                + [_const_spec(t) for t in tables])
    mixed, state = pl.pallas_call(
        _mixer_sample_kernel, grid=(batch,), in_specs=in_specs,
        out_specs=[pl.BlockSpec((length, ATTN_WIDTH + RET_WIDTH), tile), state_spec],
        out_shape=[jax.ShapeDtypeStruct((batch * length, ATTN_WIDTH + RET_WIDTH), BF16),
                   jax.ShapeDtypeStruct(state_in.shape, F32)],
        compiler_params=pltpu.CompilerParams(dimension_semantics=("parallel",),
                                             vmem_limit_bytes=VMEM_LIMIT_BYTES),
        name="mixer_sample")(qbd, kb, vb, cache_k, cache_v, state_in, sink_rows, qr, kr, vr, gr, *tables)
    return mixed, state


def _out_mlp_kernel(x_ref, mixed_ref, wout_ref, g2_ref, wup_ref, wdown_ref, gf_ref, y_ref, *,
                    ff_block, final_norm):
    h = x_ref[...] + jnp.dot(mixed_ref[...], wout_ref[...], preferred_element_type=F32)
    a = _rms_scale(h, g2_ref[...]).astype(BF16)
    acc = h
    for c in range(wup_ref.shape[1] // ff_block):
        cols = slice(c * ff_block, (c + 1) * ff_block)
        u = jnp.maximum(jnp.dot(a, wup_ref[:, cols], preferred_element_type=F32), 0.0)
        acc = acc + jnp.dot((u * u).astype(BF16), wdown_ref[cols, :], preferred_element_type=F32)
    y_ref[...] = _rms_scale(acc, gf_ref[...]) if final_norm else acc


def _out_mlp(x2d, mixed, w_out, g2, w_up, w_down, gf, tm, ff_block, final_norm):
    n, d = x2d.shape
    row = lambda i: (i, 0)
    const = lambda i: (0, 0)
    resident = lambda arr: pl.BlockSpec(arr.shape, const, pipeline_mode=pl.Buffered(1))
    in_specs = [pl.BlockSpec((tm, d), row), pl.BlockSpec((tm, mixed.shape[1]), row), resident(w_out),
                pl.BlockSpec((1, d), const), resident(w_up), resident(w_down), pl.BlockSpec((1, d), const)]
    return pl.pallas_call(
        functools.partial(_out_mlp_kernel, ff_block=ff_block, final_norm=final_norm),
        grid=(n // tm,), in_specs=in_specs, out_specs=pl.BlockSpec((tm, d), row),
        out_shape=jax.ShapeDtypeStruct((n, d), F32),
        compiler_params=pltpu.CompilerParams(dimension_semantics=("parallel",),
                                             vmem_limit_bytes=VMEM_LIMIT_BYTES),
        name="out_mlp")(x2d, mixed, w_out, g2, w_up, w_down, gf)


def _rotary_tables(pos, head_dim, theta_exponents):
    ang = pos.astype(np.float64)[:, None] / theta_exponents[None, :]
    cos, sin = np.cos(ang), np.sin(ang)
    reps = LANES // head_dim
    return (np.tile(np.concatenate([cos, cos], axis=-1), (1, reps)),
            np.tile(np.concatenate([-sin, sin], axis=-1), (1, reps)))


def _all_rotary_tables(pos, row_reps=1):
    attn = _rotary_tables(pos, HEAD_DIM, ROPE_THETA ** (np.arange(0, HEAD_DIM, 2, dtype=np.float64) / HEAD_DIM))
    ret = _rotary_tables(pos, RET_HEAD_DIM, RET_THETA ** np.linspace(0.0, 1.0, RET_HEAD_DIM // 2))
    return tuple(jnp.asarray(np.tile(t, (row_reps, 1)), dtype=F32) for t in attn + ret)


def _permute_heads(w, axis):
    shape = w.shape
    split = shape[:axis] + (N_KV_HEADS, GROUP, HEAD_DIM) + shape[axis + 1:]
    return jnp.swapaxes(w.reshape(split), axis, axis + 1).reshape(shape)


def _split_in_proj(w_in):
    bounds = [0, ATTN_WIDTH, ATTN_WIDTH + KV_WIDTH, ATTN_WIDTH + 2 * KV_WIDTH]
    bounds += [bounds[-1] + RET_WIDTH * i for i in range(1, 5)]
    return [w_in[:, lo:hi] for lo, hi in zip(bounds[:-1], bounds[1:])]


def kernel(x_prompt, x_sample, cache_k, cache_v, state_ret, norm1, w_in, sinks, w_out,
           norm2, w_up, w_down, norm_f):
    batch, seq, d = x_prompt.shape
    dec_batch, dec_seq, _ = x_sample.shape
    depth = norm1.shape[0]
    win_cache = cache_k.shape[2]
    past_len = PAST_LEN
    tm_prompt, tm_sample, tq, ff_block = 512, 512, 256, 1024
    assert dec_seq == CHUNK and win_cache == WINDOW and seq % tm_prompt == 0 and seq % tq == 0
    assert (dec_batch * dec_seq) % tm_sample == 0 and tm_sample % dec_seq == 0

    tabs_p = _all_rotary_tables(np.arange(seq))
    tabs_s = _all_rotary_tables(past_len + np.arange(dec_seq), row_reps=tm_sample // dec_seq)

    hp = x_prompt.reshape(batch * seq, d)
    hs = x_sample.reshape(dec_batch * dec_seq, d)
    outs = {name: [] for name in ("kp", "vp", "sp", "ks", "vs", "ss")}
    for l in range(depth):
        qa_w, ka_w, va_w, qr_w, kr_w, vr_w, gr_w = _split_in_proj(w_in[l])
        w_cat = jnp.concatenate([_permute_heads(qa_w, 1), ka_w, qr_w, kr_w, va_w, vr_w, gr_w],
                                axis=1).astype(BF16)
        w_out_l = jnp.concatenate([_permute_heads(w_out[l, :ATTN_WIDTH], 0), w_out[l, ATTN_WIDTH:]],
                                  axis=0).astype(BF16)
        w_up_l, w_down_l = w_up[l].astype(BF16), w_down[l].astype(BF16)
        g1, g2, gf = norm1[l][None, :], norm2[l][None, :], norm_f[None, :]
        sink_rows = jnp.repeat(sinks[l], CHUNK)[None, :]
        final = l == depth - 1

        streams = _in_proj(hp, g1, w_cat, tabs_p, tm_prompt, seq, win_cache)
        mixed, state_p = _mixer_prompt(streams, sink_rows, batch, seq, tq)
        hp = _out_mlp(hp, mixed, w_out_l, g2, w_up_l, w_down_l, gf, tm_prompt, ff_block, final)
        outs["kp"].append(streams[1].reshape(batch, win_cache, N_KV_HEADS, HEAD_DIM))
        outs["vp"].append(streams[2].reshape(batch, win_cache, N_KV_HEADS, HEAD_DIM))
        outs["sp"].append(state_p)

        streams = _in_proj(hs, g1, w_cat, tabs_s, tm_sample, dec_seq, dec_seq)
        ck = cache_k[l].reshape(dec_batch, win_cache, KV_WIDTH)
        cv = cache_v[l].reshape(dec_batch, win_cache, KV_WIDTH)
        mixed, state_s = _mixer_sample(streams, sink_rows, ck, cv, state_ret[l], dec_batch, dec_seq)
        hs = _out_mlp(hs, mixed, w_out_l, g2, w_up_l, w_down_l, gf, tm_sample, ff_block, final)
        k_all = jnp.concatenate([ck, streams[1].reshape(dec_batch, dec_seq, KV_WIDTH)], axis=1)
        v_all = jnp.concatenate([cv, streams[2].reshape(dec_batch, dec_seq, KV_WIDTH)], axis=1)
        outs["ks"].append(k_all[:, -win_cache:].reshape(dec_batch, win_cache, N_KV_HEADS, HEAD_DIM))
        outs["vs"].append(v_all[:, -win_cache:].reshape(dec_batch, win_cache, N_KV_HEADS, HEAD_DIM))
        outs["ss"].append(state_s)

    return (hp.reshape(batch, seq, d), hs.reshape(dec_batch, dec_seq, d),
            jnp.stack(outs["kp"], 0), jnp.stack(outs["vp"], 0), jnp.stack(outs["sp"], 0),
            jnp.stack(outs["ks"], 0), jnp.stack(outs["vs"], 0), jnp.stack(outs["ss"], 0))
```

```python
import functools

import jax
import jax.numpy as jnp
import numpy as np
from jax import lax
from jax.experimental import pallas as pl
from jax.experimental.pallas import tpu as pltpu

CHUNK = 64
HEAD_DIM = 64
N_HEADS = 8
N_KV_HEADS = 2
GROUP = N_HEADS // N_KV_HEADS
ATTN_WIDTH = N_HEADS * HEAD_DIM
KV_WIDTH = N_KV_HEADS * HEAD_DIM
WINDOW = 128
PAST_LEN = 2048
ROPE_THETA = 10000.0
RET_HEADS = 4
RET_HEAD_DIM = 128
RET_WIDTH = RET_HEADS * RET_HEAD_DIM
RET_THETA = 10000.0
EPS = 1e-6

LANES = 128
VMEM_LIMIT_BYTES = 56 * 1024 * 1024

ROT_A = ATTN_WIDTH + KV_WIDTH
SEC_A = ROT_A + KV_WIDTH
SEC_R = 2 * RET_WIDTH
SEC_P = 2 * RET_WIDTH

BF16 = jnp.bfloat16
F32 = jnp.float32


def _rms_scale(x, gain):
    return x * lax.rsqrt(jnp.mean(x * x, axis=-1, keepdims=True) + EPS) * gain


def _in_proj_kernel(x_ref, gain_ref, w_ref, cosa_ref, sina_ref, cosr_ref, sinr_ref,
                    qbd_ref, ktail_ref, vtail_ref, kb_ref, vb_ref,
                    qr_ref, kr_ref, vr_ref, gr_ref):
    tm = x_ref.shape[0]
    tail = tm - ktail_ref.shape[0]
    a = _rms_scale(x_ref[...], gain_ref[...]).astype(BF16)
    lane = lax.broadcasted_iota(jnp.int32, (tm, LANES), 1)
    kv0_lanes = lane < HEAD_DIM
    n_chunks = tm // CHUNK

    za = jnp.dot(a, w_ref[:, 0:SEC_A], preferred_element_type=F32)
    first_half = (lane % HEAD_DIM) < (HEAD_DIM // 2)
    cosa, sina = cosa_ref[...], sina_ref[...]
    v = za[:, ROT_A:]
    vtail_ref[...] = v[tail:, :]
    vb_ref[...] = v.astype(BF16)
    for j in range(ROT_A // LANES):
        xb = za[:, j * LANES:(j + 1) * LANES]
        partner = jnp.where(first_half, pltpu.roll(xb, LANES - HEAD_DIM // 2, 1),
                            pltpu.roll(xb, HEAD_DIM // 2, 1))
        rot = xb * cosa + partner * sina
        if j < ATTN_WIDTH // LANES:
            q = rot * (HEAD_DIM ** -0.5)
            for kv, qm in enumerate((jnp.where(kv0_lanes, q, 0.0), jnp.where(kv0_lanes, 0.0, q))):
                qbd_ref[:, kv * GROUP + j, :, :] = qm.astype(BF16).reshape(n_chunks, CHUNK, LANES)
        else:
            ktail_ref[...] = rot[tail:, :]
            kb_ref[...] = rot.astype(BF16)

    zr = jnp.dot(a, w_ref[:, SEC_A:SEC_A + SEC_R], preferred_element_type=F32)
    cosr, sinr = cosr_ref[...], sinr_ref[...]
    for j in range(SEC_R // LANES):
        xb = zr[:, j * LANES:(j + 1) * LANES]
        rot = xb * cosr + pltpu.roll(xb, RET_HEAD_DIM // 2, 1) * sinr
        if j < RET_HEADS:
            qr_ref[:, j * LANES:(j + 1) * LANES] = rot.astype(BF16)
        else:
            jj = j - RET_HEADS
            kr_ref[:, jj * LANES:(jj + 1) * LANES] = (rot * (RET_HEAD_DIM ** -0.5)).astype(BF16)

    zp = jnp.dot(a, w_ref[:, SEC_A + SEC_R:], preferred_element_type=F32)
    vr_ref[...] = zp[:, :RET_WIDTH].astype(BF16)
    gr_ref[...] = zp[:, RET_WIDTH:].astype(BF16)


def _in_proj(x2d, gain, w_cat, tables, tm, seq_len, tail_rows):
    n, d = x2d.shape
    tiles_per_seq = max(seq_len // tm, 1)
    seqs_per_tile = max(tm // seq_len, 1)
    tail_block = seqs_per_tile * tail_rows
    assert tail_block == tail_rows or tail_rows == seq_len
    row = lambda i: (i, 0)
    const = lambda i: (0, 0)
    tab = lambda i: (i % tiles_per_seq, 0)
    tail = lambda i: (i // tiles_per_seq, 0)
    wide = lambda dt: jax.ShapeDtypeStruct((n, ATTN_WIDTH), dt)
    narrow = lambda dt: jax.ShapeDtypeStruct((n, KV_WIDTH), dt)
    tails = jax.ShapeDtypeStruct((n // seq_len * tail_rows, KV_WIDTH), F32)
    qbd = jax.ShapeDtypeStruct((n // CHUNK, N_HEADS, CHUNK, LANES), BF16)
    out_shape = [qbd, tails, tails, narrow(BF16), narrow(BF16), wide(BF16), wide(BF16), wide(BF16), wide(BF16)]
    out_specs = [pl.BlockSpec((tm // CHUNK, N_HEADS, CHUNK, LANES), lambda i: (i, 0, 0, 0)) if s is qbd
                 else pl.BlockSpec((tail_block, KV_WIDTH), tail) if s is tails
                 else pl.BlockSpec((tm, s.shape[1]), row) for s in out_shape]
    in_specs = [pl.BlockSpec((tm, d), row), pl.BlockSpec((1, d), const),
                pl.BlockSpec(w_cat.shape, const)] + [pl.BlockSpec((tm, LANES), tab)] * 4
    return pl.pallas_call(
        _in_proj_kernel, grid=(n // tm,), in_specs=in_specs, out_specs=out_specs, out_shape=out_shape,
        compiler_params=pltpu.CompilerParams(dimension_semantics=("arbitrary",),
                                             vmem_limit_bytes=VMEM_LIMIT_BYTES),
        name="in_proj")(x2d, gain, w_cat, *tables)


def _attend_chunks(chunks, sink_ref):
    sink = sink_ref[...]
    contract_last = (((1,), (1,)), ((), ()))
    contract_rows = (((0,), (0,)), ((), ()))
    scores = [lax.dot_general(k, q, contract_last, preferred_element_type=F32) for q, k, _, _ in chunks]
    probs = []
    for (_, _, _, n_valid), s in zip(chunks, scores):
        if n_valid is not None:
            s = jnp.where(lax.broadcasted_iota(jnp.int32, s.shape, 0) < n_valid, s, -jnp.inf)
        m = jnp.maximum(jnp.max(s, axis=0, keepdims=True), sink)
        p = jnp.exp(s - m)
        denom = jnp.sum(p, axis=0, keepdims=True) + jnp.exp(sink - m)
        probs.append((p.astype(BF16), 1.0 / denom))
    outs = []
    half = GROUP * CHUNK
    for (_, _, v, _), (p, inv) in zip(chunks, probs):
        ot = lax.dot_general(v, p, contract_rows, preferred_element_type=F32) * inv
        own = jnp.concatenate([ot[:HEAD_DIM, :half], ot[HEAD_DIM:, half:]], axis=0)
        outs.append(own.T)
    return outs


def _store_attn(mixed_ref, row0, o):
    for g in range(GROUP):
        mixed_ref[row0:row0 + CHUNK, g * LANES:(g + 1) * LANES] = (
            o[g * CHUNK:(g + 1) * CHUNK, :].astype(mixed_ref.dtype))


def _retention(qr_ref, kr_ref, vr_ref, gr_ref, dmask_ref, qdec_ref, kdec_ref, gl_ref, states, mixed_ref):
    block = dmask_ref.shape[1]
    heads = range(RET_HEADS)
    cols = [slice(h * RET_HEAD_DIM, (h + 1) * RET_HEAD_DIM) for h in heads]
    contract_last = (((1,), (1,)), ((), ()))
    contract_rows = (((0,), (0,)), ((), ()))
    for b in range(qr_ref.shape[0] // block):
        rows = slice(b * block, (b + 1) * block)
        sc = [lax.dot_general(qr_ref[rows, c], kr_ref[rows, c], contract_last, preferred_element_type=F32)
              for c in cols]
        cross = [jnp.dot(qr_ref[rows, c], st.astype(BF16), preferred_element_type=F32)
                 for c, st in zip(cols, states)]
        kd = [(kr_ref[rows, c].astype(F32) * kdec_ref[:, c]).astype(BF16) for c in cols]
        kv = [lax.dot_general(k, vr_ref[rows, c], contract_rows, preferred_element_type=F32)
              for k, c in zip(kd, cols)]
        intra = [jnp.dot((s * dmask_ref[h]).astype(BF16), vr_ref[rows, c], preferred_element_type=F32)
                 for h, (s, c) in enumerate(zip(sc, cols))]
        states = [gl_ref[h] * states[h] + kv[h] for h in heads]
        for h in heads:
            o = intra[h] + cross[h] * qdec_ref[:, cols[h]]
            r = o * lax.rsqrt(jnp.mean(o * o, axis=-1, keepdims=True) + EPS)
            g = gr_ref[rows, cols[h]].astype(F32)
            gate = g / (1.0 + jnp.exp(-g))
            mixed_ref[rows, ATTN_WIDTH + h * RET_HEAD_DIM:ATTN_WIDTH + (h + 1) * RET_HEAD_DIM] = (
                (r * gate).astype(mixed_ref.dtype))
    return states


def _mixer_prompt_kernel(qbd_ref, kb_ref, vb_ref, sink_ref,
                         qr_ref, kr_ref, vr_ref, gr_ref, dmask_ref, qdec_ref, kdec_ref, gl_ref,
                         mixed_ref, state_out_ref, state_scr):
    j = pl.program_id(1)
    chunks = qbd_ref.shape[0]
    span = WINDOW + CHUNK
    past = WINDOW // CHUNK

    @pl.when(j == 0)
    def _():
        state_scr[...] = jnp.zeros_like(state_scr)

    work = []
    for c in range(chunks):
        cg = j * chunks + c
        start = pl.multiple_of(jnp.maximum(cg - past, 0) * CHUNK, CHUNK)
        n_valid = jnp.minimum(cg + 1, past + 1) * CHUNK
        win = pl.ds(start, span)
        work.append((qbd_ref[c].reshape(N_HEADS * CHUNK, LANES), kb_ref[win, :], vb_ref[win, :], n_valid))
    for c, o in enumerate(_attend_chunks(work, sink_ref)):
        _store_attn(mixed_ref, c * CHUNK, o)

    states = _retention(qr_ref, kr_ref, vr_ref, gr_ref, dmask_ref, qdec_ref, kdec_ref, gl_ref,
                        [state_scr[h] for h in range(RET_HEADS)], mixed_ref)
    for h, st in enumerate(states):
        state_scr[h] = st

    @pl.when(j == pl.num_programs(1) - 1)
    def _():
        state_out_ref[0] = state_scr[...]


def _mixer_sample_kernel(qbd_ref, knew_ref, vnew_ref, kcache_ref, vcache_ref, state_in_ref, sink_ref,
                         qr_ref, kr_ref, vr_ref, gr_ref, dmask_ref, qdec_ref, kdec_ref, gl_ref,
                         mixed_ref, state_out_ref):
    k = jnp.concatenate([kcache_ref[0].astype(BF16), knew_ref[...]], axis=0)
    v = jnp.concatenate([vcache_ref[0].astype(BF16), vnew_ref[...]], axis=0)
    (o,) = _attend_chunks([(qbd_ref[0].reshape(N_HEADS * CHUNK, LANES), k, v, None)], sink_ref)
    _store_attn(mixed_ref, 0, o)

    states = _retention(qr_ref, kr_ref, vr_ref, gr_ref, dmask_ref, qdec_ref, kdec_ref, gl_ref,
                        [state_in_ref[0, h] for h in range(RET_HEADS)], mixed_ref)
    for h, st in enumerate(states):
        state_out_ref[0, h] = st


def _retention_tables(length):
    log_g = np.log1p(-np.exp2(-5.0 - np.arange(RET_HEADS, dtype=np.float64)))
    idx = np.arange(length, dtype=np.float64)
    diff = idx[:, None] - idx[None, :]
    dmask = np.where(diff >= 0, np.exp(log_g[:, None, None] * np.maximum(diff, 0.0)), 0.0)
    qdec = np.exp(log_g[None, :] * (idx[:, None] + 1.0))
    kdec = np.exp(log_g[None, :] * (length - 1.0 - idx[:, None]))
    gl = np.exp(log_g * length)
    tables = (dmask, np.repeat(qdec, RET_HEAD_DIM, axis=1), np.repeat(kdec, RET_HEAD_DIM, axis=1),
              np.broadcast_to(gl[:, None, None], (RET_HEADS, 1, RET_HEAD_DIM)))
    return tuple(jnp.asarray(t, dtype=F32) for t in tables)


def _const_spec(arr):
    nd = arr.ndim
    return pl.BlockSpec(arr.shape, lambda *_: (0,) * nd)


def _mixer_prompt(streams, sink_rows, batch, seq, tq, ret_block):
    qbd, _, _, kb, vb, qr, kr, vr, gr = streams
    nt = seq // tq
    tables = _retention_tables(ret_block)
    tile = lambda b, j: (b * nt + j, 0)
    per_batch = lambda b, j: (b, 0)
    wide = pl.BlockSpec((tq, ATTN_WIDTH), tile)
    qspec = pl.BlockSpec((tq // CHUNK, N_HEADS, CHUNK, LANES), lambda b, j: (b * nt + j, 0, 0, 0))
    kvspec = pl.BlockSpec((seq, KV_WIDTH), per_batch)
    in_specs = ([qspec] + [kvspec] * 2 + [_const_spec(sink_rows)] + [wide] * 4
                + [_const_spec(t) for t in tables])
    mixed, state = pl.pallas_call(
        _mixer_prompt_kernel, grid=(batch, nt), in_specs=in_specs,
        out_specs=[pl.BlockSpec((tq, ATTN_WIDTH + RET_WIDTH), tile),
                   pl.BlockSpec((1, RET_HEADS, RET_HEAD_DIM, RET_HEAD_DIM), lambda b, j: (b, 0, 0, 0))],
        out_shape=[jax.ShapeDtypeStruct((batch * seq, ATTN_WIDTH + RET_WIDTH), BF16),
                   jax.ShapeDtypeStruct((batch, RET_HEADS, RET_HEAD_DIM, RET_HEAD_DIM), F32)],
        scratch_shapes=[pltpu.VMEM((RET_HEADS, RET_HEAD_DIM, RET_HEAD_DIM), F32)],
        compiler_params=pltpu.CompilerParams(dimension_semantics=("parallel", "arbitrary"),
                                             vmem_limit_bytes=VMEM_LIMIT_BYTES),
        name="mixer_prompt")(qbd, kb, vb, sink_rows, qr, kr, vr, gr, *tables)
    return mixed, state


def _mixer_sample(streams, sink_rows, cache_k, cache_v, state_in, batch, length):
    qbd, _, _, kb, vb, qr, kr, vr, gr = streams
    tables = _retention_tables(length)
    tile = lambda b: (b, 0)
    wide = pl.BlockSpec((length, ATTN_WIDTH), tile)
    qspec = pl.BlockSpec((length // CHUNK, N_HEADS, CHUNK, LANES), lambda b: (b, 0, 0, 0))
    new_kv = pl.BlockSpec((length, KV_WIDTH), tile)
    cache = pl.BlockSpec((1,) + cache_k.shape[1:], lambda b: (b, 0, 0))
    state_spec = pl.BlockSpec((1, RET_HEADS, RET_HEAD_DIM, RET_HEAD_DIM), lambda b: (b, 0, 0, 0))
    in_specs = ([qspec, new_kv, new_kv, cache, cache, state_spec, _const_spec(sink_rows)] + [wide] * 4
                + [_const_spec(t) for t in tables])
    mixed, state = pl.pallas_call(
        _mixer_sample_kernel, grid=(batch,), in_specs=in_specs,
        out_specs=[pl.BlockSpec((length, ATTN_WIDTH + RET_WIDTH), tile), state_spec],
        out_shape=[jax.ShapeDtypeStruct((batch * length, ATTN_WIDTH + RET_WIDTH), BF16),
                   jax.ShapeDtypeStruct(state_in.shape, F32)],
        compiler_params=pltpu.CompilerParams(dimension_semantics=("parallel",),
                                             vmem_limit_bytes=VMEM_LIMIT_BYTES),
        name="mixer_sample")(qbd, kb, vb, cache_k, cache_v, state_in, sink_rows, qr, kr, vr, gr, *tables)
    return mixed, state


def _out_mlp_kernel(x_ref, mixed_ref, wout_ref, g2_ref, wup_ref, wdown_ref, gf_ref, y_ref, *,
                    ff_block, final_norm):
    h = x_ref[...] + jnp.dot(mixed_ref[...], wout_ref[...], preferred_element_type=F32)
    a = _rms_scale(h, g2_ref[...]).astype(BF16)
    acc = h
    for c in range(wup_ref.shape[1] // ff_block):
        cols = slice(c * ff_block, (c + 1) * ff_block)
        u = jnp.maximum(jnp.dot(a, wup_ref[:, cols], preferred_element_type=F32), 0.0)
        acc = acc + jnp.dot((u * u).astype(BF16), wdown_ref[cols, :], preferred_element_type=F32)
    y_ref[...] = _rms_scale(acc, gf_ref[...]) if final_norm else acc


def _out_mlp(x2d, mixed, w_out, g2, w_up, w_down, gf, tm, ff_block, final_norm):
    n, d = x2d.shape
    row = lambda i: (i, 0)
    const = lambda i: (0, 0)
    resident = lambda arr: pl.BlockSpec(arr.shape, const, pipeline_mode=pl.Buffered(1))
    in_specs = [pl.BlockSpec((tm, d), row), pl.BlockSpec((tm, mixed.shape[1]), row), resident(w_out),
                pl.BlockSpec((1, d), const), resident(w_up), resident(w_down), pl.BlockSpec((1, d), const)]
    return pl.pallas_call(
        functools.partial(_out_mlp_kernel, ff_block=ff_block, final_norm=final_norm),
        grid=(n // tm,), in_specs=in_specs, out_specs=pl.BlockSpec((tm, d), row),
        out_shape=jax.ShapeDtypeStruct((n, d), F32),
        compiler_params=pltpu.CompilerParams(dimension_semantics=("parallel",),
                                             vmem_limit_bytes=VMEM_LIMIT_BYTES),
        name="out_mlp")(x2d, mixed, w_out, g2, w_up, w_down, gf)


def _rotary_tables(pos, head_dim, theta_exponents):
    ang = pos.astype(np.float64)[:, None] / theta_exponents[None, :]
    cos, sin = np.cos(ang), np.sin(ang)
    reps = LANES // head_dim
    return (np.tile(np.concatenate([cos, cos], axis=-1), (1, reps)),
            np.tile(np.concatenate([-sin, sin], axis=-1), (1, reps)))


def _all_rotary_tables(pos, row_reps=1):
    attn = _rotary_tables(pos, HEAD_DIM, ROPE_THETA ** (np.arange(0, HEAD_DIM, 2, dtype=np.float64) / HEAD_DIM))
    ret = _rotary_tables(pos, RET_HEAD_DIM, RET_THETA ** np.linspace(0.0, 1.0, RET_HEAD_DIM // 2))
    return tuple(jnp.asarray(np.tile(t, (row_reps, 1)), dtype=F32) for t in attn + ret)


def _permute_heads(w, axis):
    shape = w.shape
    split = shape[:axis] + (N_KV_HEADS, GROUP, HEAD_DIM) + shape[axis + 1:]
    return jnp.swapaxes(w.reshape(split), axis, axis + 1).reshape(shape)


def kernel(x_prompt, x_sample, cache_k, cache_v, state_ret, norm1, w_in, sinks, w_out,
           norm2, w_up, w_down, norm_f):
    batch, seq, d = x_prompt.shape
    dec_batch, dec_seq, _ = x_sample.shape
    depth = norm1.shape[0]
    win_cache = cache_k.shape[2]
    past_len = PAST_LEN
    tm_prompt, tm_sample, tq, ret_block, ff_block = 512, 512, 512, 256, 1024
    assert dec_seq == CHUNK and win_cache == WINDOW and seq % tm_prompt == 0 and seq % tq == 0
    assert tq % ret_block == 0
    assert (dec_batch * dec_seq) % tm_sample == 0 and tm_sample % dec_seq == 0

    tabs_p = _all_rotary_tables(np.arange(seq))
    tabs_s = _all_rotary_tables(past_len + np.arange(dec_seq), row_reps=tm_sample // dec_seq)

    hp = x_prompt.reshape(batch * seq, d)
    hs = x_sample.reshape(dec_batch * dec_seq, d)
    outs = {name: [] for name in ("kp", "vp", "sp", "ks", "vs", "ss")}
    for l in range(depth):
        w_cat = jnp.concatenate([_permute_heads(w_in[l, :, :ATTN_WIDTH], 1), w_in[l, :, ATTN_WIDTH:]],
                                axis=1).astype(BF16)
        w_out_l = jnp.concatenate([_permute_heads(w_out[l, :ATTN_WIDTH], 0), w_out[l, ATTN_WIDTH:]],
                                  axis=0).astype(BF16)
        w_up_l, w_down_l = w_up[l].astype(BF16), w_down[l].astype(BF16)
        g1, g2, gf = norm1[l][None, :], norm2[l][None, :], norm_f[None, :]
        sink_rows = jnp.repeat(sinks[l], CHUNK)[None, :]
        final = l == depth - 1

        streams = _in_proj(hp, g1, w_cat, tabs_p, tm_prompt, seq, win_cache)
        mixed, state_p = _mixer_prompt(streams, sink_rows, batch, seq, tq, ret_block)
        hp = _out_mlp(hp, mixed, w_out_l, g2, w_up_l, w_down_l, gf, tm_prompt, ff_block, final)
        outs["kp"].append(streams[1].reshape(batch, win_cache, N_KV_HEADS, HEAD_DIM))
        outs["vp"].append(streams[2].reshape(batch, win_cache, N_KV_HEADS, HEAD_DIM))
        outs["sp"].append(state_p)

        streams = _in_proj(hs, g1, w_cat, tabs_s, tm_sample, dec_seq, dec_seq)
        ck = cache_k[l].reshape(dec_batch, win_cache, KV_WIDTH)
        cv = cache_v[l].reshape(dec_batch, win_cache, KV_WIDTH)
        mixed, state_s = _mixer_sample(streams, sink_rows, ck, cv, state_ret[l], dec_batch, dec_seq)
        hs = _out_mlp(hs, mixed, w_out_l, g2, w_up_l, w_down_l, gf, tm_sample, ff_block, final)
        k_all = jnp.concatenate([ck, streams[1].reshape(dec_batch, dec_seq, KV_WIDTH)], axis=1)
        v_all = jnp.concatenate([cv, streams[2].reshape(dec_batch, dec_seq, KV_WIDTH)], axis=1)
        outs["ks"].append(k_all[:, -win_cache:].reshape(dec_batch, win_cache, N_KV_HEADS, HEAD_DIM))
        outs["vs"].append(v_all[:, -win_cache:].reshape(dec_batch, win_cache, N_KV_HEADS, HEAD_DIM))
        outs["ss"].append(state_s)

    return (hp.reshape(batch, seq, d), hs.reshape(dec_batch, dec_seq, d),
            jnp.stack(outs["kp"], 0), jnp.stack(outs["vp"], 0), jnp.stack(outs["sp"], 0),
            jnp.stack(outs["ks"], 0), jnp.stack(outs["vs"], 0), jnp.stack(outs["ss"], 0))
```
